```python
import jax, jax.numpy as jnp
from jax import lax
import numpy as np

D_MODEL = 1024
BATCH = 16
SEQ = 2048
DEPTH = 2

GRID_W = 64
CTX_LEN = 256
N_MIXERS = 2
N_HGRN_LAYERS = (DEPTH + 1) // 2
N_MLSTM_LAYERS = DEPTH // 2
EPS = 1e-6
HG_HEADS = 8
HG_DK = D_MODEL // HG_HEADS
HG_DV = D_MODEL // HG_HEADS
HG_CHUNK = 64
ML_HEADS = 8
ML_QK = D_MODEL // 2
ML_V = D_MODEL
ML_DK = ML_QK // ML_HEADS
ML_DV = ML_V // ML_HEADS
ML_CHUNK = 64
GATE_CAP = 15.0
CONV_K = 3
FGATE_BIAS_LO = 3.0
FGATE_BIAS_HI = 6.0
N_EXPERTS = 32
TOP_K = 4
D_FF = D_MODEL
SWIGLU_LIMIT = 7.0
SWIGLU_ALPHA = 1.702
MOE_BLOCK = 256

kernel_name = "hybrid_hgrn2_mlstm_moe_flow_backbone"


def _split_heads(a, n_heads):
    b, l, _ = a.shape
    return a.reshape(b, l, n_heads, -1).transpose(0, 2, 1, 3)


def _merge_heads(a):
    b, h, l, d = a.shape
    return a.transpose(0, 2, 1, 3).reshape(b, l, h * d)


def _to_chunks(a, size):
    b, h, l = a.shape[:3]
    a = a.reshape(b, h, l // size, size, *a.shape[3:])
    return jnp.moveaxis(a, 2, 0)


def _from_chunks(a):
    n, b, h, size = a.shape[:4]
    return jnp.moveaxis(a, 0, 2).reshape(b, h, n * size, *a.shape[4:])


def rms_norm(x, g):
    x32 = x.astype(jnp.float32)
    y = x32 * lax.rsqrt(jnp.mean(x32 * x32, axis=-1, keepdims=True) + EPS)
    return (y * g.astype(jnp.float32)).astype(x.dtype)


def head_rms_norm(a, g, n_heads):
    b, l, d = a.shape
    a32 = a.astype(jnp.float32).reshape(b, l, n_heads, d // n_heads)
    a32 = a32 * lax.rsqrt(jnp.mean(a32 * a32, axis=-1, keepdims=True) + EPS)
    return a32.reshape(b, l, d) * g.astype(jnp.float32)


def modulate(x, g, shift, scale):
    return rms_norm(x, g) * (1 + scale) + shift


def run_direction(scan_fn, ctx_args, lat_args, init, reverse):
    flip = (lambda a: jnp.flip(a, axis=2)) if reverse else (lambda a: a)
    o_ctx, ctx_state = scan_fn(*[flip(a) for a in ctx_args], init)
    o_lat, _ = scan_fn(*[flip(a) for a in lat_args], ctx_state)
    return flip(o_ctx), flip(o_lat)


def hgrn2_scan(q, k, v, logf, s0):
    q, k, v, logf = [a.astype(jnp.float32) for a in (q, k, v, logf)]
    size = HG_CHUNK
    mid = size // 2
    tril = jnp.tril(jnp.ones((size, size), bool))

    def body(s, xs):
        qc, kc, vc, lc = xs
        b = jnp.cumsum(lc, axis=2)
        b_ref = b[:, :, mid:mid + 1]
        att = jnp.einsum('bhtd,bhsd->bhts', qc * jnp.exp(b - b_ref), kc * jnp.exp(b_ref - b))
        att = jnp.where(tril, att, 0.0)
        o = jnp.einsum('bhts,bhse->bhte', att, vc) + jnp.einsum('bhtd,bhde->bhte', qc * jnp.exp(b), s)
        b_end = b[:, :, -1:]
        s_new = jnp.exp(b_end[:, :, 0])[..., None] * s + jnp.einsum('bhsd,bhse->bhde', kc * jnp.exp(b_end - b), vc)
        return s_new, o

    xs = tuple(_to_chunks(a, size) for a in (q, k, v, logf))
    s_final, o = lax.scan(body, s0.astype(jnp.float32), xs)
    return _from_chunks(o), s_final


def mlstm_scan(q, k, v, ig, logf, state):
    q, k, v, ig, logf = [a.astype(jnp.float32) for a in (q, k, v, ig, logf)]
    size = ML_CHUNK
    tril = jnp.tril(jnp.ones((size, size), bool))

    def body(carry, xs):
        c0, n0, m0 = carry
        qc, kc, vc, ic, fc = xs
        b = jnp.cumsum(fc, axis=-1)
        dmat = jnp.where(tril, b[..., :, None] - b[..., None, :] + ic[..., None, :], -jnp.inf)
        inter_log = b + m0[..., None]
        m_t = jnp.maximum(inter_log, jnp.max(dmat, axis=-1))
        w_intra = jnp.exp(dmat - m_t[..., None])
        w_inter = jnp.exp(inter_log - m_t)
        sc = jnp.einsum('bhtd,bhsd->bhts', qc, kc) * w_intra
        num = jnp.einsum('bhts,bhse->bhte', sc, vc) + w_inter[..., None] * jnp.einsum('bhtd,bhde->bhte', qc, c0)
        den = jnp.sum(sc, axis=-1) + w_inter * jnp.einsum('bhtd,bhd->bht', qc, n0)
        h = num / jnp.maximum(jnp.abs(den), jnp.exp(-m_t))[..., None]
        b_end = b[..., -1]
        g = b_end[..., None] - b + ic
        m_new = jnp.maximum(b_end + m0, jnp.max(g, axis=-1))
        wk = jnp.exp(g - m_new[..., None])
        decay = jnp.exp(b_end + m0 - m_new)
        c_new = decay[..., None, None] * c0 + jnp.einsum('bhs,bhsd,bhse->bhde', wk, kc, vc)
        n_new = decay[..., None] * n0 + jnp.einsum('bhs,bhsd->bhd', wk, kc)
        return (c_new, n_new, m_new), h

    xs = tuple(_to_chunks(a, size) for a in (q, k, v, ig, logf))
    final, h = lax.scan(body, state, xs)
    return _from_chunks(h), final


def short_conv(u, w, bias, on_grid):
    b, l, ch = u.shape
    if on_grid:
        rows = l // GRID_W
        y = lax.conv_general_dilated(u.reshape(b, rows, GRID_W, ch), w[:, :, None, :], (1, 1), 'SAME',
                                     dimension_numbers=('NHWC', 'HWIO', 'NHWC'), feature_group_count=ch)
        y = y.reshape(b, l, ch)
    else:
        y = lax.conv_general_dilated(u, w[CONV_K // 2][:, None, :], (1,), 'SAME',
                                     dimension_numbers=('NWC', 'WIO', 'NWC'), feature_group_count=ch)
    return jax.nn.silu(y + bias)


def hgrn2_mixer(h_ctx, h_lat, w_in, lb_fwd, lb_bwd, norm_g, w_out, need_ctx):
    def prep(h):
        q, i, zf, zb, g = jnp.split(h @ w_in, 5, axis=-1)
        f_f = lb_fwd + (1.0 - lb_fwd) * jax.nn.sigmoid(zf.astype(jnp.float32))
        f_b = lb_bwd + (1.0 - lb_bwd) * jax.nn.sigmoid(zb.astype(jnp.float32))
        sh = lambda a: _split_heads(a, HG_HEADS)
        fwd = (sh(jax.nn.silu(q)), sh(1.0 - f_f), sh(i), sh(jnp.log(f_f)))
        bwd = (sh(jax.nn.silu(q)), sh(1.0 - f_b), sh(i), sh(jnp.log(f_b)))
        return fwd, bwd, g

    fwd_c, bwd_c, g_c = prep(h_ctx)
    fwd_l, bwd_l, g_l = prep(h_lat)
    s0 = jnp.zeros((h_lat.shape[0], HG_HEADS, HG_DK, HG_DV), jnp.float32)
    of_c, of_l = run_direction(hgrn2_scan, fwd_c, fwd_l, s0, False)
    ob_c, ob_l = run_direction(hgrn2_scan, bwd_c, bwd_l, s0, True)

    def readout(o, g, dtype):
        y = head_rms_norm(_merge_heads(o), norm_g, HG_HEADS) * jax.nn.silu(g.astype(jnp.float32))
        return y.astype(dtype) @ w_out

    y_lat = readout(of_l + ob_l, g_l, h_lat.dtype)
    y_ctx = readout(of_c + ob_c, g_c, h_ctx.dtype) if need_ctx else None
    return y_ctx, y_lat


def mlstm_mixer(h_ctx, h_lat, w_in, b_gate, conv_w, conv_b, norm_g, w_out, need_ctx):
    def prep(h, on_grid):
        qk, v, o, gates = jnp.split(h @ w_in, [2 * ML_QK, 2 * ML_QK + ML_V, 2 * ML_QK + 2 * ML_V], axis=-1)
        q, k = jnp.split(short_conv(qk, conv_w, conv_b, on_grid), 2, axis=-1)
        gates = GATE_CAP * jnp.tanh((gates.astype(jnp.float32) + b_gate) / GATE_CAP)
        i_f, i_b, f_f, f_b = [a.transpose(0, 2, 1) for a in jnp.split(gates, 4, axis=-1)]
        qh = _split_heads(q, ML_HEADS) * (ML_DK ** -0.5)
        kh = _split_heads(k, ML_HEADS)
        vh = _split_heads(v, ML_HEADS)
        fwd = (qh, kh, vh, i_f, jax.nn.log_sigmoid(f_f))
        bwd = (qh, kh, vh, i_b, jax.nn.log_sigmoid(f_b))
        return fwd, bwd, o

    fwd_c, bwd_c, o_c = prep(h_ctx, False)
    fwd_l, bwd_l, o_l = prep(h_lat, True)
    bsz = h_lat.shape[0]
    init = (jnp.zeros((bsz, ML_HEADS, ML_DK, ML_DV), jnp.float32),
            jnp.zeros((bsz, ML_HEADS, ML_DK), jnp.float32),
            jnp.zeros((bsz, ML_HEADS), jnp.float32))
    hf_c, hf_l = run_direction(mlstm_scan, fwd_c, fwd_l, init, False)
    hb_c, hb_l = run_direction(mlstm_scan, bwd_c, bwd_l, init, True)

    def readout(hh, o, dtype):
        y = head_rms_norm(_merge_heads(hh), norm_g, ML_HEADS) * jax.nn.sigmoid(o.astype(jnp.float32))
        return y.astype(dtype) @ w_out

    y_lat = readout(hf_l + hb_l, o_l, h_lat.dtype)
    y_ctx = readout(hf_c + hb_c, o_c, h_ctx.dtype) if need_ctx else None
    return y_ctx, y_lat


def moe_ffn(h, router_w, router_b, w1, b1, w2, b2):
    n_tok, d = h.shape
    logits = (h @ router_w + router_b).astype(jnp.float32)
    top_v, top_e = lax.top_k(logits, TOP_K)
    gates = jax.nn.softmax(top_v, axis=-1)
    e_flat = top_e.reshape(-1)
    g_flat = gates.reshape(-1)
    tok_flat = jnp.repeat(jnp.arange(n_tok, dtype=jnp.int32), TOP_K)
    n_slots = n_tok * TOP_K
    order = jnp.argsort(e_flat)
    e_sorted = e_flat[order]
    counts = jnp.bincount(e_flat, length=N_EXPERTS)
    starts = jnp.cumsum(counts) - counts
    padded = (counts + MOE_BLOCK - 1) // MOE_BLOCK * MOE_BLOCK
    pends = jnp.cumsum(padded)
    pstarts = pends - padded
    dest = pstarts[e_sorted] + (jnp.arange(n_slots) - starts[e_sorted])
    n_rows = -(-(n_slots + N_EXPERTS * (MOE_BLOCK - 1)) // MOE_BLOCK) * MOE_BLOCK
    n_blocks = n_rows // MOE_BLOCK
    row_tok = jnp.full((n_rows,), n_tok, jnp.int32).at[dest].set(tok_flat[order])
    row_gate = jnp.zeros((n_rows,), jnp.float32).at[dest].set(g_flat[order])
    h_pad = jnp.concatenate([h, jnp.zeros((1, d), h.dtype)], axis=0)
    xs = h_pad[row_tok].reshape(n_blocks, MOE_BLOCK, d)
    block_e = jnp.minimum(jnp.searchsorted(pends, jnp.arange(n_blocks) * MOE_BLOCK, side='right'), N_EXPERTS - 1)

    def expert_block(args):
        xb, e = args
        u = xb @ w1[e] + b1[e]
        glu, lin = jnp.split(u, 2, axis=-1)
        glu = jnp.minimum(glu, SWIGLU_LIMIT)
        lin = jnp.clip(lin, -SWIGLU_LIMIT, SWIGLU_LIMIT)
        y = glu * jax.nn.sigmoid(SWIGLU_ALPHA * glu) * (lin + 1.0)
        return y @ w2[e] + b2[e]

    ys = lax.map(expert_block, (xs, block_e)).reshape(n_rows, d)
    out = jax.ops.segment_sum(ys * row_gate[:, None], row_tok, num_segments=n_tok + 1)[:n_tok]
    return out.astype(h.dtype)


def setup_inputs(seed: int = 0) -> dict:
    key = jax.random.key(seed)
    ks = iter(jax.random.split(key, 32))

    def nrm(shape, scale):
        return jax.random.normal(next(ks), shape, jnp.float32) * scale

    d = D_MODEL
    ml_in = 2 * ML_QK + 2 * ML_V + 4 * ML_HEADS
    f_bias = jnp.tile(jnp.linspace(FGATE_BIAS_LO, FGATE_BIAS_HI, ML_HEADS), 2)
    ml_b_gate = jnp.concatenate([jnp.zeros((2 * ML_HEADS,), jnp.float32), f_bias])[None, :] + nrm((N_MLSTM_LAYERS, 4 * ML_HEADS), 0.1)
    return {
        'x': nrm((BATCH, SEQ, d), 1.0),
        'c': nrm((BATCH, d), 1.0),
        'ctx': nrm((BATCH, CTX_LEN, d), 1.0),
        'c_ctx': nrm((d,), 1.0),
        'ada_w': nrm((DEPTH, d, 6 * d), 0.5 * d ** -0.5),
        'ada_b': nrm((DEPTH, 6 * d), 0.02),
        'norm_mix_g': 1.0 + nrm((DEPTH, d), 0.02),
        'norm_ffn_g': 1.0 + nrm((DEPTH, d), 0.02),
        'hg_w_in': nrm((N_HGRN_LAYERS, d, 5 * d), d ** -0.5),
        'hg_lb_logits': nrm((2, DEPTH + 1, d), 0.1),
        'hg_norm_g': 1.0 + nrm((N_HGRN_LAYERS, d), 0.02),
        'hg_w_out': nrm((N_HGRN_LAYERS, d, d), d ** -0.5),
        'ml_w_in': nrm((N_MLSTM_LAYERS, d, ml_in), d ** -0.5),
        'ml_b_gate': ml_b_gate,
        'ml_conv_w': nrm((N_MLSTM_LAYERS, CONV_K, CONV_K, 2 * ML_QK), 1.0 / CONV_K),
        'ml_conv_b': nrm((N_MLSTM_LAYERS, 2 * ML_QK), 0.02),
        'ml_norm_g': 1.0 + nrm((N_MLSTM_LAYERS, ML_V), 0.02),
        'ml_w_out': nrm((N_MLSTM_LAYERS, ML_V, d), ML_V ** -0.5),
        'router_w': nrm((DEPTH, d, N_EXPERTS), d ** -0.5),
        'router_b': nrm((DEPTH, N_EXPERTS), 0.01),
        'moe_w1': nrm((DEPTH, N_EXPERTS, d, 2 * D_FF), d ** -0.5),
        'moe_b1': nrm((DEPTH, N_EXPERTS, 2 * D_FF), 0.01),
        'moe_w2': nrm((DEPTH, N_EXPERTS, D_FF, d), D_FF ** -0.5),
        'moe_b2': nrm((DEPTH, N_EXPERTS, d), 0.01),
        'final_norm_g': 1.0 + nrm((d,), 0.02),
    }


def reference(x, c, ctx, c_ctx, ada_w, ada_b, norm_mix_g, norm_ffn_g, hg_w_in, hg_lb_logits, hg_norm_g, hg_w_out,
              ml_w_in, ml_b_gate, ml_conv_w, ml_conv_b, ml_norm_g, ml_w_out, router_w, router_b,
              moe_w1, moe_b1, moe_w2, moe_b2, final_norm_g):
    d = D_MODEL
    c_act = jax.nn.silu(c)
    cc_act = jax.nn.silu(c_ctx)
    lb_all = jnp.cumsum(jax.nn.softmax(hg_lb_logits.astype(jnp.float32), axis=1), axis=1)
    for layer in range(DEPTH):
        need_ctx = layer < DEPTH - 1
        j = layer // N_MIXERS
        mod_l = jnp.split((c_act @ ada_w[layer] + ada_b[layer])[:, None, :], 6, axis=-1)
        mod_c = jnp.split(cc_act @ ada_w[layer] + ada_b[layer], 6, axis=-1)
        h_lat = modulate(x, norm_mix_g[layer], mod_l[0], mod_l[1])
        h_ctx = modulate(ctx, norm_mix_g[layer], mod_c[0], mod_c[1])
        if layer % N_MIXERS == 0:
            y_ctx, y_lat = hgrn2_mixer(h_ctx, h_lat, hg_w_in[j], lb_all[0, layer], lb_all[1, layer],
                                       hg_norm_g[j], hg_w_out[j], need_ctx)
        else:
            y_ctx, y_lat = mlstm_mixer(h_ctx, h_lat, ml_w_in[j], ml_b_gate[j], ml_conv_w[j], ml_conv_b[j],
                                       ml_norm_g[j], ml_w_out[j], need_ctx)
        x = x + mod_l[2] * y_lat
        f_lat = modulate(x, norm_ffn_g[layer], mod_l[3], mod_l[4]).reshape(-1, d)
        moe_args = (router_w[layer], router_b[layer], moe_w1[layer], moe_b1[layer], moe_w2[layer], moe_b2[layer])
        if need_ctx:
            ctx = ctx + mod_c[2] * y_ctx
            f_ctx = modulate(ctx, norm_ffn_g[layer], mod_c[3], mod_c[4]).reshape(-1, d)
            out = moe_ffn(jnp.concatenate([f_ctx, f_lat], axis=0), *moe_args)
            n_ctx_tok = f_ctx.shape[0]
            ctx = ctx + mod_c[5] * out[:n_ctx_tok].reshape(ctx.shape)
            out_lat = out[n_ctx_tok:]
        else:
            out_lat = moe_ffn(f_lat, *moe_args)
        x = x + mod_l[5] * out_lat.reshape(x.shape)
    return rms_norm(x, final_norm_g)
```

```python
import functools

import jax
import jax.numpy as jnp
from jax import lax
from jax.experimental import pallas as pl
from jax.experimental.pallas import tpu as pltpu

F32 = jnp.float32
BF16 = jnp.bfloat16
I32 = jnp.int32

EPS = 1e-6
HEADS = 8
CHUNK = 64
GATE_CAP = 15.0
GRID_W = 64
N_EXPERTS = 32
TOP_K = 4
MOE_BLOCK = 256
SWIGLU_LIMIT = 7.0
SWIGLU_ALPHA = 1.702

TM = 256
ADA_TN = 512
VMEM_LIMIT = 56 * 1024 * 1024


def _params(sem, vmem=None):
    return pltpu.CompilerParams(dimension_semantics=sem, vmem_limit_bytes=vmem)


def _dot(a, b):
    return jnp.dot(a, b, preferred_element_type=F32)


def _dot_nt(a, b):
    return lax.dot_general(a, b, (((1,), (1,)), ((), ())), preferred_element_type=F32)


def _dot_tn(a, b):
    return lax.dot_general(a, b, (((0,), (0,)), ((), ())), preferred_element_type=F32)


def _split2(x):
    hi = x.astype(BF16)
    lo = (x - hi.astype(F32)).astype(BF16)
    return hi, lo


def _split3(x):
    p1 = x.astype(BF16)
    r1 = x - p1.astype(F32)
    p2 = r1.astype(BF16)
    p3 = (r1 - p2.astype(F32)).astype(BF16)
    return p1, p2, p3


def _dot_hi(a, w):
    a1, a2 = _split2(a)
    w1, w2 = _split2(w)
    return _dot(a1, w1) + (_dot(a1, w2) + _dot(a2, w1))


def _dot_exact_lhs(t, x):
    p1, p2, p3 = _split3(x)
    return _dot(t, p1) + (_dot(t, p2) + _dot(t, p3))


def _sigmoid(x):
    return jax.nn.sigmoid(x)


def _silu(x):
    return x * jax.nn.sigmoid(x)


def _rms(x, g):
    return x * lax.rsqrt(jnp.mean(x * x, axis=-1, keepdims=True) + EPS) * g


def _modulate(x, g, shift, scale):
    return _rms(x, g) * (1.0 + scale) + shift


def _causal(n, reverse):
    r = lax.broadcasted_iota(I32, (n, n), 0)
    s = lax.broadcasted_iota(I32, (n, n), 1)
    return (s >= r) if reverse else (s <= r)


def _ada_body(c_ref, w_ref, b_ref, o_ref):
    a = _silu(c_ref[...])
    o_ref[0] = _dot_hi(a, w_ref[0]) + b_ref[0]


def _ada(cstack, ada_w, ada_b):
    depth, d, n6 = ada_w.shape
    nb = cstack.shape[0]
    return pl.pallas_call(
        _ada_body,
        grid=(depth, n6 // ADA_TN),
        in_specs=[
            pl.BlockSpec((nb, d), lambda l, j: (0, 0)),
            pl.BlockSpec((1, d, ADA_TN), lambda l, j: (l, 0, j)),
            pl.BlockSpec((1, 1, ADA_TN), lambda l, j: (l, 0, j)),
        ],
        out_specs=pl.BlockSpec((1, nb, ADA_TN), lambda l, j: (l, 0, j)),
        out_shape=jax.ShapeDtypeStruct((depth, nb, n6), F32),
        compiler_params=_params(("arbitrary", "arbitrary")),
        name="ada",
    )(cstack, ada_w, ada_b.reshape(depth, 1, n6))


def _hg_prep_body(x_ref, mod_ref, g_ref, w_ref, lb_ref, q_o, v_o, kf_o, kb_o, lf_o, lbw_o, g_o, *, layer, d):
    mod = mod_ref[0, 0]
    h = _modulate(x_ref[0], g_ref[...], mod[0:1], mod[1:2]).astype(BF16)

    def lower_bound(direction):
        rows = [lb_ref[direction, r : r + 1, :] for r in range(lb_ref.shape[1])]
        mx = functools.reduce(jnp.maximum, rows)
        es = [jnp.exp(r - mx) for r in rows]
        return sum(es[: layer + 1]) / sum(es)

    def proj(j):
        return _dot(h, w_ref[:, j * d : (j + 1) * d])

    q_o[0] = _silu(proj(0)).astype(BF16)
    v_o[0] = proj(1).astype(BF16)
    for j, k_o, l_o in ((2, kf_o, lf_o), (3, kb_o, lbw_o)):
        lb = lower_bound(j - 2)
        f = lb + (1.0 - lb) * _sigmoid(proj(j))
        k_o[0] = (1.0 - f).astype(BF16)
        l_o[0] = jnp.log(f)
    g_o[0] = proj(4).astype(BF16)


def _hg_prep(xc, mods, norm_g, w_in, lb_logits, layer, nct):
    nb, lc, d = xc.shape
    tile = pl.BlockSpec((1, TM, d), lambda b, i: (b, i, 0))
    out = lambda dt: jax.ShapeDtypeStruct((nb, lc, d), dt)
    return pl.pallas_call(
        functools.partial(_hg_prep_body, layer=layer, d=d),
        grid=(nb, lc // TM),
        in_specs=[
            tile,
            pl.BlockSpec((1, 1, 6, d), lambda b, i: (layer, jnp.where(i < nct, nb, b), 0, 0)),
            pl.BlockSpec((1, d), lambda b, i: (0, 0)),
            pl.BlockSpec((d, 5 * d), lambda b, i: (0, 0)),
            pl.BlockSpec(lb_logits.shape, lambda b, i: (0, 0, 0)),
        ],
        out_specs=[tile] * 7,
        out_shape=[out(BF16), out(BF16), out(BF16), out(BF16), out(F32), out(F32), out(BF16)],
        compiler_params=_params(("arbitrary", "arbitrary"), VMEM_LIMIT),
        name="hg_prep",
    )(xc, mods, norm_g.reshape(1, d), w_in, lb_logits)


def _hg_dir(q_ref, v_ref, k_ref, l_ref, o_ref, s_ref, reverse):
    c = CHUNK
    causal = _causal(c, reverse)
    bc = _dot_exact_lhs(causal.astype(BF16), l_ref[0])
    mid = (c - 1 - c // 2) if reverse else c // 2
    end = 0 if reverse else c - 1
    b_mid = bc[mid : mid + 1]
    b_end = bc[end : end + 1]
    q = q_ref[0].astype(F32)
    k = k_ref[0].astype(F32)
    v = v_ref[0]
    qd = (q * jnp.exp(bc - b_mid)).astype(BF16)
    kd = (k * jnp.exp(b_mid - bc)).astype(BF16)
    qs = (q * jnp.exp(bc)).astype(BF16)
    ke = (k * jnp.exp(b_end - bc)).astype(BF16)
    decay = jnp.exp(b_end)
    dh = q.shape[-1] // HEADS
    for h in range(HEADS):
        sl = slice(h * dh, (h + 1) * dh)
        att = jnp.where(causal, _dot_nt(qd[:, sl], kd[:, sl]), 0.0).astype(BF16)
        st = s_ref[h]
        o_ref[0, :, sl] = _dot(att, v[:, sl]) + _dot_nt(qs[:, sl], st.astype(BF16))
        s_ref[h] = st * decay[:, sl] + _dot_tn(v[:, sl], ke[:, sl])


def _hg_scan_body(qf, vf, kf, lf, qb, vb, kb, lb, of, ob, sf, sb):
    @pl.when(pl.program_id(1) == 0)
    def _():
        sf[...] = jnp.zeros_like(sf)
        sb[...] = jnp.zeros_like(sb)

    _hg_dir(qf, vf, kf, lf, of, sf, False)
    _hg_dir(qb, vb, kb, lb, ob, sb, True)


def _chunk_maps(ncc, nc):
    fwd = lambda b, c: (b, c, 0)
    bwd = lambda b, c: (b, jnp.where(c < ncc, ncc - 1 - c, nc - 1 + ncc - c), 0)
    return fwd, bwd


def _hg_scan(q, v, kf, kb, lf, lb, ncc):
    nb, lc, d = q.shape
    nc = lc // CHUNK
    fwd, bwd = _chunk_maps(ncc, nc)
    blk = lambda m: pl.BlockSpec((1, CHUNK, d), m)
    dh = d // HEADS
    return pl.pallas_call(
        _hg_scan_body,
        grid=(nb, nc),
        in_specs=[blk(fwd)] * 4 + [blk(bwd)] * 4,
        out_specs=[blk(fwd), blk(bwd)],
        out_shape=[jax.ShapeDtypeStruct((nb, lc, d), F32)] * 2,
        scratch_shapes=[pltpu.VMEM((HEADS, dh, dh), F32)] * 2,
        compiler_params=_params(("arbitrary", "arbitrary")),
        name="hg_scan",
    )(q, v, kf, lf, q, v, kb, lb)


def _post_body(of, ob, g_ref, x_ref, mod_ref, ng_ref, wo_ref, nf_ref, rw_ref, rb_ref, x_o, f_o, ti_o, tg_o, *, gate):
    y = of[0] + ob[0]
    dh = y.shape[-1] // HEADS
    parts = []
    for h in range(HEADS):
        yh = y[:, h * dh : (h + 1) * dh]
        parts.append(yh * lax.rsqrt(jnp.mean(yh * yh, axis=-1, keepdims=True) + EPS))
    gt = g_ref[0].astype(F32)
    act = _silu(gt) if gate == "silu" else _sigmoid(gt)
    yn = jnp.concatenate(parts, axis=-1) * ng_ref[...] * act
    mod = mod_ref[0, 0]
    xn = x_ref[0] + mod[2:3] * _dot(yn.astype(BF16), wo_ref[...])
    x_o[0] = xn
    f = _modulate(xn, nf_ref[...], mod[3:4], mod[4:5])
    f_o[0] = f
    vals = _dot_hi(f, rw_ref[...]) + rb_ref[...]
    lane = lax.broadcasted_iota(I32, vals.shape, 1)
    tops, ids = [], []
    for _ in range(TOP_K):
        m = jnp.max(vals, axis=-1, keepdims=True)
        idx = jnp.min(jnp.where(vals == m, lane, N_EXPERTS), axis=-1, keepdims=True)
        tops.append(m)
        ids.append(idx)
        vals = jnp.where(lane == idx, -jnp.inf, vals)
    es = [jnp.exp(m - tops[0]) for m in tops]
    tot = functools.reduce(lambda a, b: a + b, es)
    tg_o[0] = jnp.concatenate([e / tot for e in es], axis=-1)
    ti_o[0] = jnp.concatenate(ids, axis=-1)


def _post(of, ob, g, xc, mods, norm_g, w_out, norm_ffn_g, router_w, router_b, layer, nct, skip, gate):
    nb, lc, d = xc.shape
    nt = lc // TM - skip
    tile = pl.BlockSpec((1, TM, d), lambda b, i: (b, i + skip, 0))
    otile = pl.BlockSpec((1, TM, d), lambda b, i: (b, i, 0))
    small = pl.BlockSpec((1, TM, TOP_K), lambda b, i: (b, i, 0))
    row = pl.BlockSpec((1, d), lambda b, i: (0, 0))
    return pl.pallas_call(
        functools.partial(_post_body, gate=gate),
        grid=(nb, nt),
        in_specs=[
            tile, tile, tile, tile,
            pl.BlockSpec((1, 1, 6, d), lambda b, i: (layer, jnp.where(i + skip < nct, nb, b), 0, 0)),
            row,
            pl.BlockSpec((d, d), lambda b, i: (0, 0)),
            row,
            pl.BlockSpec((d, N_EXPERTS), lambda b, i: (0, 0)),
            pl.BlockSpec((1, N_EXPERTS), lambda b, i: (0, 0)),
        ],
        out_specs=[tile, otile, small, small],
        out_shape=[
            jax.ShapeDtypeStruct((nb, lc, d), F32),
            jax.ShapeDtypeStruct((nb, nt * TM, d), F32),
            jax.ShapeDtypeStruct((nb, nt * TM, TOP_K), I32),
            jax.ShapeDtypeStruct((nb, nt * TM, TOP_K), F32),
        ],
        input_output_aliases={3: 0},
        compiler_params=_params(("arbitrary", "arbitrary")),
        name="post_mixer",
    )(of, ob, g, xc, mods, norm_g.reshape(1, d), w_out, norm_ffn_g.reshape(1, d), router_w,
      router_b.reshape(1, N_EXPERTS))


def _rank_body(ti_ref, rank_o, cnt_o, carry):
    @pl.when(pl.program_id(0) == 0)
    def _():
        carry[...] = jnp.zeros_like(carry)

    ti = ti_ref[...]
    lane = lax.broadcasted_iota(I32, (TM, N_EXPERTS), 1)
    hots = [lane == ti[:, k : k + 1] for k in range(TOP_K)]
    chosen = functools.reduce(lambda a, b: a + b, [h.astype(F32) for h in hots])
    r = lax.broadcasted_iota(I32, (TM, TM), 0)
    s = lax.broadcasted_iota(I32, (TM, TM), 1)
    before = _dot((s < r).astype(BF16), chosen.astype(BF16)) + carry[...]
    ranks = [jnp.sum(jnp.where(h, before, 0.0), axis=-1, keepdims=True) for h in hots]
    rank_o[...] = jnp.concatenate(ranks, axis=-1).astype(I32)
    carry[...] += jnp.sum(chosen, axis=0, keepdims=True)
    cnt_o[...] = carry[...]


def _rank(ti):
    t = ti.shape[0]
    return pl.pallas_call(
        _rank_body,
        grid=(t // TM,),
        in_specs=[pl.BlockSpec((TM, TOP_K), lambda i: (i, 0))],
        out_specs=[pl.BlockSpec((TM, TOP_K), lambda i: (i, 0)), pl.BlockSpec((1, N_EXPERTS), lambda i: (0, 0))],
        out_shape=[jax.ShapeDtypeStruct((t, TOP_K), I32), jax.ShapeDtypeStruct((1, N_EXPERTS), F32)],
        scratch_shapes=[pltpu.VMEM((1, N_EXPERTS), F32)],
        compiler_params=_params(("arbitrary",)),
        name="moe_rank",
    )(ti)


def _row_copy(src, i, dst, j, sem):
    return pltpu.make_async_copy(src.at[pl.ds(i, 1)], dst.at[pl.ds(j, 1)], sem)


def _dispatch_body(ps_ref, cnt_ref, pe_ref, ti_ref, rk_ref, f_hbm, xs_hbm, zrow, sem, zsem, *, n_rows, n_pad):
    i = pl.program_id(0)
    base = i * TM

    def issue(s, carry):
        dst = ps_ref[ti_ref[s]] + rk_ref[s]
        _row_copy(f_hbm, base + lax.shift_right_logical(s, 2), xs_hbm, dst, sem).start()
        return carry

    lax.fori_loop(0, TM * TOP_K, issue, 0, unroll=8)

    @pl.when(i == 0)
    def _():
        zrow[...] = jnp.zeros_like(zrow)

        def zero(r, carry):
            _row_copy(zrow, 0, xs_hbm, r, zsem).start()
            return carry

        def per_expert(e, carry):
            lax.fori_loop(ps_ref[e] + cnt_ref[e], pe_ref[e], zero, 0)
            return carry

        lax.fori_loop(0, N_EXPERTS, per_expert, 0)
        lax.fori_loop(pe_ref[N_EXPERTS - 1], n_rows, zero, 0)

        def wait_zero(r, carry):
            _row_copy(zrow, 0, xs_hbm, 0, zsem).wait()
            return carry

        lax.fori_loop(0, n_pad, wait_zero, 0)

    pltpu.make_async_copy(f_hbm.at[pl.ds(0, TM * TOP_K)], xs_hbm.at[pl.ds(0, TM * TOP_K)], sem).wait()


def _dispatch(pstarts, counts, pends, ti_flat, rk_flat, f, n_rows):
    t, d = f.shape
    n_slots = TM * TOP_K
    return pl.pallas_call(
        functools.partial(_dispatch_body, n_rows=n_rows, n_pad=n_rows - t * TOP_K),
        grid_spec=pltpu.PrefetchScalarGridSpec(
            num_scalar_prefetch=3,
            grid=(t // TM,),
            in_specs=[
                pl.BlockSpec((n_slots,), lambda i, *_: (i,), memory_space=pltpu.SMEM),
                pl.BlockSpec((n_slots,), lambda i, *_: (i,), memory_space=pltpu.SMEM),
                pl.BlockSpec(memory_space=pl.ANY),
            ],
            out_specs=pl.BlockSpec(memory_space=pl.ANY),
            scratch_shapes=[pltpu.VMEM((8, d), F32), pltpu.SemaphoreType.DMA, pltpu.SemaphoreType.DMA],
        ),
        out_shape=jax.ShapeDtypeStruct((n_rows, d), F32),
        compiler_params=_params(("arbitrary",)),
        name="moe_dispatch",
    )(pstarts, counts, pends, ti_flat, rk_flat, f)


def _ffn_body(be_ref, nu_ref, xs_ref, w1_ref, b1_ref, w2_ref, b2_ref, ys_ref, w1b, w2b):
    b = pl.program_id(0)
    changed = jnp.logical_or(b == 0, be_ref[b] != be_ref[jnp.maximum(b - 1, 0)])

    @pl.when(changed)
    def _():
        w1b[...] = w1_ref[0].astype(BF16)
        w2b[...] = w2_ref[0].astype(BF16)

    @pl.when(b < nu_ref[0])
    def _():
        u = _dot(xs_ref[...].astype(BF16), w1b[...]) + b1_ref[0]
        dff = u.shape[-1] // 2
        glu = jnp.minimum(u[:, :dff], SWIGLU_LIMIT)
        lin = jnp.clip(u[:, dff:], -SWIGLU_LIMIT, SWIGLU_LIMIT)
        y = glu * _sigmoid(SWIGLU_ALPHA * glu) * (lin + 1.0)
        ys_ref[...] = _dot(y.astype(BF16), w2b[...]) + b2_ref[0]

    @pl.when(b >= nu_ref[0])
    def _():
        ys_ref[...] = jnp.zeros_like(ys_ref)


def _ffn(block_e, n_used, xs, w1, b1, w2, b2):
    n_rows, d = xs.shape
    ne, _, dff2 = w1.shape
    return pl.pallas_call(
        _ffn_body,
        grid_spec=pltpu.PrefetchScalarGridSpec(
            num_scalar_prefetch=2,
            grid=(n_rows // MOE_BLOCK,),
            in_specs=[
                pl.BlockSpec((MOE_BLOCK, d), lambda b, be, nu: (b, 0)),
                pl.BlockSpec((1, d, dff2), lambda b, be, nu: (be[b], 0, 0)),
                pl.BlockSpec((1, 1, dff2), lambda b, be, nu: (be[b], 0, 0)),
                pl.BlockSpec((1, dff2 // 2, d), lambda b, be, nu: (be[b], 0, 0)),
                pl.BlockSpec((1, 1, d), lambda b, be, nu: (be[b], 0, 0)),
            ],
            out_specs=pl.BlockSpec((MOE_BLOCK, d), lambda b, be, nu: (b, 0)),
            scratch_shapes=[pltpu.VMEM((d, dff2), BF16), pltpu.VMEM((dff2 // 2, d), BF16)],
        ),
        out_shape=jax.ShapeDtypeStruct((n_rows, d), F32),
        compiler_params=_params(("arbitrary",), VMEM_LIMIT),
        name="moe_ffn",
    )(block_e, n_used, xs, w1, b1.reshape(ne, 1, dff2), w2, b2.reshape(ne, 1, d))


def _combine_body(ps_ref, ti_ref, rk_ref, tg_ref, ys_hbm, x_ref, mod_ref, fg_ref, o_ref, buf, sem, *, final):
    def issue(s, carry):
        src = ps_ref[ti_ref[s]] + rk_ref[s]
        row = (s & (TOP_K - 1)) * TM + lax.shift_right_logical(s, 2)
        _row_copy(ys_hbm, src, buf, row, sem).start()
        return carry

    lax.fori_loop(0, TM * TOP_K, issue, 0, unroll=8)
    pltpu.make_async_copy(ys_hbm.at[pl.ds(0, TM * TOP_K)], buf, sem).wait()
    g = tg_ref[0]
    acc = g[:, 0:1] * buf[0:TM]
    for k in range(1, TOP_K):
        acc += g[:, k : k + 1] * buf[k * TM : (k + 1) * TM]
    xn = x_ref[0] + mod_ref[0, 0][5:6] * acc
    o_ref[0] = _rms(xn, fg_ref[...]) if final else xn


def _combine(pstarts, ti_flat, rk_flat, tg, ys, xc, mods, final_g, layer, nct, skip, final):
    nb, lc, d = xc.shape
    nt = lc // TM - skip
    n_slots = TM * TOP_K
    tile = pl.BlockSpec((1, TM, d), lambda b, i, *_: (b, i + skip, 0))
    if final:
        out_spec = pl.BlockSpec((1, TM, d), lambda b, i, *_: (b, i, 0))
        out_shape = jax.ShapeDtypeStruct((nb, nt * TM, d), F32)
        aliases = {}
    else:
        out_spec, out_shape, aliases = tile, jax.ShapeDtypeStruct((nb, lc, d), F32), {5: 0}
    return pl.pallas_call(
        functools.partial(_combine_body, final=final),
        grid_spec=pltpu.PrefetchScalarGridSpec(
            num_scalar_prefetch=1,
            grid=(nb, nt),
            in_specs=[
                pl.BlockSpec((n_slots,), lambda b, i, *_: (b * nt + i,), memory_space=pltpu.SMEM),
                pl.BlockSpec((n_slots,), lambda b, i, *_: (b * nt + i,), memory_space=pltpu.SMEM),
                pl.BlockSpec((1, TM, TOP_K), lambda b, i, *_: (b, i, 0)),
                pl.BlockSpec(memory_space=pl.ANY),
                tile,
                pl.BlockSpec((1, 1, 6, d), lambda b, i, *_: (layer, jnp.where(i + skip < nct, nb, b), 0, 0)),
                pl.BlockSpec((1, d), lambda b, i, *_: (0, 0)),
            ],
            out_specs=out_spec,
            scratch_shapes=[pltpu.VMEM((n_slots, d), F32), pltpu.SemaphoreType.DMA],
        ),
        out_shape=out_shape,
        input_output_aliases=aliases,
        compiler_params=_params(("arbitrary", "arbitrary")),
        name="moe_combine",
    )(pstarts, ti_flat, rk_flat, tg, ys, xc, mods, final_g.reshape(1, d))


def _moe(f, ti, tg, xc, mods, w1, b1, w2, b2, final_g, layer, nct, skip, final):
    nb, ntok, d = f.shape
    t = nb * ntok
    rank, cnt = _rank(ti.reshape(t, TOP_K))
    counts = cnt[0].astype(I32)
    padded = (counts + MOE_BLOCK - 1) // MOE_BLOCK * MOE_BLOCK
    pends = jnp.cumsum(padded)
    pstarts = pends - padded
    n_slots = t * TOP_K
    n_rows = -(-(n_slots + N_EXPERTS * (MOE_BLOCK - 1)) // MOE_BLOCK) * MOE_BLOCK
    n_blocks = n_rows // MOE_BLOCK
    block_e = jnp.minimum(
        jnp.searchsorted(pends, jnp.arange(n_blocks, dtype=I32) * MOE_BLOCK, side="right"), N_EXPERTS - 1
    ).astype(I32)
    n_used = (pends[-1:] // MOE_BLOCK).astype(I32)
    ti_flat = ti.reshape(n_slots)
    rk_flat = rank.reshape(n_slots)
    xs = _dispatch(pstarts, counts, pends, ti_flat, rk_flat, f.reshape(t, d), n_rows)
    ys = _ffn(block_e, n_used, xs, w1, b1, w2, b2)
    return _combine(pstarts, ti_flat, rk_flat, tg, ys, xc, mods, final_g, layer, nct, skip, final)


def _ml_prep_body(x_ref, mod_ref, g_ref, wqk_ref, wv_ref, wo_ref, wg_ref, bg_ref, qk_o, v_o, o_o, gt_o):
    mod = mod_ref[0, 0]
    h = _modulate(x_ref[0], g_ref[...], mod[0:1], mod[1:2])
    hb = h.astype(BF16)
    qk_o[0] = _dot(hb, wqk_ref[...])
    v_o[0] = _dot(hb, wv_ref[...]).astype(BF16)
    o_o[0] = _dot(hb, wo_ref[...]).astype(BF16)
    gates = GATE_CAP * jnp.tanh((_dot_hi(h, wg_ref[...]) + bg_ref[...]) / GATE_CAP)
    ng = gates.shape[-1] // 2
    fg = gates[:, ng:]
    logf = jnp.minimum(fg, 0.0) - jnp.log1p(jnp.exp(-jnp.abs(fg)))
    pad = jnp.zeros((gates.shape[0], gt_o.shape[-1] - 2 * ng), F32)
    gt_o[0] = jnp.concatenate([gates[:, :ng], logf, pad], axis=-1)


def _ml_prep(xc, mods, norm_g, wqk, wv, wo, wg, bg, layer, nct):
    nb, lc, d = xc.shape
    tile = lambda w: pl.BlockSpec((1, TM, w), lambda b, i: (b, i, 0))
    full = lambda a: pl.BlockSpec(a.shape, lambda b, i: (0,) * a.ndim)
    return pl.pallas_call(
        _ml_prep_body,
        grid=(nb, lc // TM),
        in_specs=[
            tile(d),
            pl.BlockSpec((1, 1, 6, d), lambda b, i: (layer, jnp.where(i < nct, nb, b), 0, 0)),
            pl.BlockSpec((1, d), lambda b, i: (0, 0)),
            full(wqk), full(wv), full(wo), full(wg), full(bg),
        ],
        out_specs=[tile(wqk.shape[1]), tile(wv.shape[1]), tile(wo.shape[1]), tile(128)],
        out_shape=[
            jax.ShapeDtypeStruct((nb, lc, wqk.shape[1]), F32),
            jax.ShapeDtypeStruct((nb, lc, wv.shape[1]), BF16),
            jax.ShapeDtypeStruct((nb, lc, wo.shape[1]), BF16),
            jax.ShapeDtypeStruct((nb, lc, 128), F32),
        ],
        compiler_params=_params(("arbitrary", "arbitrary"), VMEM_LIMIT),
        name="ml_prep",
    )(xc, mods, norm_g.reshape(1, d), wqk, wv, wo, wg, bg)


CONV_PAD = 72


def _conv_body(main, prev, nxt, cw_ref, cb_ref, q_o, k_o, sc, *, nct, dk):
    i = pl.program_id(1)
    last = pl.num_programs(1) - 1
    p = CONV_PAD
    w = main.shape[-1]
    sc[0:p - GRID_W] = jnp.zeros((p - GRID_W, w), F32)
    sc[p + TM + GRID_W:] = jnp.zeros((sc.shape[0] - p - TM - GRID_W, w), F32)
    sc[p - GRID_W:p] = jnp.where(i > nct, prev[0], 0.0)
    sc[p + TM:p + TM + GRID_W] = jnp.where(jnp.logical_and(i >= nct, i < last), nxt[0], 0.0)
    sc[p:p + TM] = main[0]
    col = lax.broadcasted_iota(I32, (TM, 1), 0) & (GRID_W - 1)

    def finish(acc):
        y = _silu(acc)
        q_o[0] = (y[:, :w // 2] * (dk ** -0.5)).astype(BF16)
        k_o[0] = y[:, w // 2:].astype(BF16)

    @pl.when(i < nct)
    def _():
        acc = cb_ref[...] + sc[p:p + TM] * cw_ref[4:5]
        acc += sc[p - 1:p - 1 + TM] * cw_ref[3:4] + sc[p + 1:p + 1 + TM] * cw_ref[5:6]
        finish(acc)

    @pl.when(i >= nct)
    def _():
        acc = jnp.broadcast_to(cb_ref[...], (TM, w))
        for dy in range(3):
            for dx in range(3):
                off = p + (dy - 1) * GRID_W + (dx - 1)
                u = sc[off:off + TM]
                if dx == 0:
                    u = jnp.where(col != 0, u, 0.0)
                if dx == 2:
                    u = jnp.where(col != GRID_W - 1, u, 0.0)
                acc += u * cw_ref[dy * 3 + dx:dy * 3 + dx + 1]
        finish(acc)


def _conv(qk, conv_w, conv_b, nct):
    nb, lc, w = qk.shape
    nt = lc // TM
    per = TM // GRID_W
    nh = lc // GRID_W
    return pl.pallas_call(
        functools.partial(_conv_body, nct=nct, dk=w // 2 // HEADS),
        grid=(nb, nt),
        in_specs=[
            pl.BlockSpec((1, TM, w), lambda b, i: (b, i, 0)),
            pl.BlockSpec((1, GRID_W, w), lambda b, i: (b, jnp.maximum(i * per - 1, 0), 0)),
            pl.BlockSpec((1, GRID_W, w), lambda b, i: (b, jnp.minimum(i * per + per, nh - 1), 0)),
            pl.BlockSpec((9, w), lambda b, i: (0, 0)),
            pl.BlockSpec((1, w), lambda b, i: (0, 0)),
        ],
        out_specs=[pl.BlockSpec((1, TM, w // 2), lambda b, i: (b, i, 0))] * 2,
        out_shape=[jax.ShapeDtypeStruct((nb, lc, w // 2), BF16)] * 2,
        scratch_shapes=[pltpu.VMEM((CONV_PAD + TM + GRID_W + 8, w), F32)],
        compiler_params=_params(("arbitrary", "arbitrary")),
        name="ml_conv",
    )(qk, qk, qk, conv_w.reshape(9, w), conv_b.reshape(1, w))


def _ml_dir(q_ref, k_ref, v_ref, gt_ref, o_ref, c_ref, m_ref, reverse):
    c = CHUNK
    causal = _causal(c, reverse)
    gts = gt_ref[0]
    bsum = _dot_exact_lhs(causal.astype(BF16), gts)
    gts_t = gts.T
    bsum_t = bsum.T
    io = HEADS if reverse else 0
    fo = 2 * HEADS + io
    end = 0 if reverse else c - 1
    q = q_ref[0]
    k = k_ref[0]
    v = v_ref[0]
    dk = q.shape[-1] // HEADS
    dv = v.shape[-1] // HEADS
    one_col = (lax.broadcasted_iota(I32, (c, dv), 1) == 0).astype(BF16)
    for h in range(HEADS):
        bc = bsum[:, fo + h : fo + h + 1]
        br = bsum_t[fo + h : fo + h + 1, :]
        ir = gts_t[io + h : io + h + 1, :]
        ic = gts[:, io + h : io + h + 1]
        m0 = m_ref[h][0:1, 0:1]
        dmat = jnp.where(causal, bc - br + ir, -jnp.inf)
        inter = bc + m0
        mt = jnp.maximum(inter, jnp.max(dmat, axis=-1, keepdims=True))
        w_intra = jnp.exp(dmat - mt)
        w_inter = jnp.exp(inter - mt)
        qh = q[:, h * dk : (h + 1) * dk]
        kh = k[:, h * dk : (h + 1) * dk]
        v_aug = jnp.concatenate([v[:, h * dv : (h + 1) * dv], one_col], axis=-1)
        sc = _dot_nt(qh, kh) * w_intra
        lhs = jnp.concatenate([sc, w_inter * qh.astype(F32)], axis=-1).astype(BF16)
        ca = c_ref[h]
        nd = _dot(lhs, jnp.concatenate([v_aug, ca.astype(BF16)], axis=0))
        den = nd[:, dv : dv + 1]
        o_ref[0, :, h * dv : (h + 1) * dv] = nd[:, :dv] / jnp.maximum(jnp.abs(den), jnp.exp(-mt))
        b_end = bc[end : end + 1]
        g = b_end - bc + ic
        m_new = jnp.maximum(b_end + m0, jnp.max(g, axis=0, keepdims=True))
        kw = (jnp.exp(g - m_new) * kh.astype(F32)).astype(BF16)
        c_ref[h] = jnp.exp(b_end + m0 - m_new) * ca + _dot_tn(kw, v_aug)
        m_ref[h] = jnp.broadcast_to(m_new, m_ref.shape[1:])


def _ml_scan_body(qf, kf, vf, gf, qb, kb, vb, gb, of, ob, cf, cb, mf, mb):
    @pl.when(pl.program_id(1) == 0)
    def _():
        for ref in (cf, cb, mf, mb):
            ref[...] = jnp.zeros_like(ref)

    _ml_dir(qf, kf, vf, gf, of, cf, mf, False)
    _ml_dir(qb, kb, vb, gb, ob, cb, mb, True)


def _ml_scan(q, k, v, gates, ncc):
    nb, lc, dqk = q.shape
    dvv = v.shape[-1]
    nc = lc // CHUNK
    fwd, bwd = _chunk_maps(ncc, nc)
    blk = lambda w, m: pl.BlockSpec((1, CHUNK, w), m)
    ins = lambda m: [blk(dqk, m), blk(dqk, m), blk(dvv, m), blk(gates.shape[-1], m)]
    return pl.pallas_call(
        _ml_scan_body,
        grid=(nb, nc),
        in_specs=ins(fwd) + ins(bwd),
        out_specs=[blk(dvv, fwd), blk(dvv, bwd)],
        out_shape=[jax.ShapeDtypeStruct((nb, lc, dvv), F32)] * 2,
        scratch_shapes=[pltpu.VMEM((HEADS, dqk // HEADS, 2 * dvv // HEADS), F32)] * 2
        + [pltpu.VMEM((HEADS, 8, 128), F32)] * 2,
        compiler_params=_params(("arbitrary", "arbitrary")),
        name="ml_scan",
    )(q, k, v, gates, q, k, v, gates)


def kernel(x, c, ctx, c_ctx, ada_w, ada_b, norm_mix_g, norm_ffn_g, hg_w_in, hg_lb_logits, hg_norm_g, hg_w_out,
           ml_w_in, ml_b_gate, ml_conv_w, ml_conv_b, ml_norm_g, ml_w_out, router_w, router_b,
           moe_w1, moe_b1, moe_w2, moe_b2, final_norm_g):
    nb, seq, d = x.shape
    ctx_len = ctx.shape[1]
    depth = ada_w.shape[0]
    assert ctx_len == TM and seq % TM == 0 and seq % GRID_W == 0 and d % (HEADS * 128) == 0
    nct = ctx_len // TM
    ncc = ctx_len // CHUNK

    xc = jnp.concatenate([ctx, x], axis=1)
    rows = -(-(nb + 1) // 8) * 8
    cstack = jnp.concatenate([c, c_ctx[None, :], jnp.zeros((rows - nb - 1, d), F32)], axis=0)
    mods = _ada(cstack, ada_w, ada_b).reshape(depth, rows, 6, d)

    for layer in range(depth):
        j = layer // 2
        last = layer == depth - 1
        if layer % 2 == 0:
            q, v, kf, kb, lf, lb, g = _hg_prep(xc, mods, norm_mix_g[layer], hg_w_in[j].astype(BF16),
                                               hg_lb_logits, layer, nct)
            of, ob = _hg_scan(q, v, kf, kb, lf, lb, ncc)
            norm_g, w_out, gate = hg_norm_g[j], hg_w_out[j], "silu"
        else:
            w = ml_w_in[j]
            dqk = ml_conv_w.shape[-1]
            dv = ml_w_out.shape[1]
            qk, v, g, gates = _ml_prep(
                xc, mods, norm_mix_g[layer], w[:, :dqk].astype(BF16), w[:, dqk:dqk + dv].astype(BF16),
                w[:, dqk + dv:dqk + 2 * dv].astype(BF16), w[:, dqk + 2 * dv:], ml_b_gate[j][None, :], layer, nct)
            q, k = _conv(qk, ml_conv_w[j], ml_conv_b[j], nct)
            of, ob = _ml_scan(q, k, v, gates, ncc)
            norm_g, w_out, gate = ml_norm_g[j], ml_w_out[j], "sigmoid"
        skip = nct if last else 0
        xc, f, ti, tg = _post(of, ob, g, xc, mods, norm_g, w_out.astype(BF16), norm_ffn_g[layer],
                              router_w[layer], router_b[layer], layer, nct, skip, gate)
        xc = _moe(f, ti, tg, xc, mods, moe_w1[layer], moe_b1[layer], moe_w2[layer], moe_b2[layer],
                  final_norm_g, layer, nct, skip, last)
    return xc
```

```python
import functools

import jax
import jax.numpy as jnp
from jax import lax
from jax.experimental import pallas as pl
from jax.experimental.pallas import tpu as pltpu

F32 = jnp.float32
BF16 = jnp.bfloat16
I32 = jnp.int32

EPS = 1e-6
HEADS = 8
CHUNK = 64
GATE_CAP = 15.0
GRID_W = 64
N_EXPERTS = 32
TOP_K = 4
MOE_BLOCK = 256
SWIGLU_LIMIT = 7.0
SWIGLU_ALPHA = 1.702

TM = 256
ADA_TN = 512
VMEM_LIMIT = 56 * 1024 * 1024


def _params(sem, vmem=None):
    return pltpu.CompilerParams(dimension_semantics=sem, vmem_limit_bytes=vmem)


def _dot(a, b):
    return jnp.dot(a, b, preferred_element_type=F32)


def _dot_nt(a, b):
    return lax.dot_general(a, b, (((1,), (1,)), ((), ())), preferred_element_type=F32)


def _dot_tn(a, b):
    return lax.dot_general(a, b, (((0,), (0,)), ((), ())), preferred_element_type=F32)


def _split2(x):
    hi = x.astype(BF16)
    lo = (x - hi.astype(F32)).astype(BF16)
    return hi, lo


def _split3(x):
    p1 = x.astype(BF16)
    r1 = x - p1.astype(F32)
    p2 = r1.astype(BF16)
    p3 = (r1 - p2.astype(F32)).astype(BF16)
    return p1, p2, p3


def _dot_hi(a, w):
    a1, a2 = _split2(a)
    w1, w2 = _split2(w)
    return _dot(a1, w1) + (_dot(a1, w2) + _dot(a2, w1))


def _dot_exact_lhs(t, x):
    p1, p2, p3 = _split3(x)
    return _dot(t, p1) + (_dot(t, p2) + _dot(t, p3))


def _sigmoid(x):
    return jax.nn.sigmoid(x)


def _silu(x):
    return x * jax.nn.sigmoid(x)


def _rms(x, g):
    return x * lax.rsqrt(jnp.mean(x * x, axis=-1, keepdims=True) + EPS) * g


def _modulate(x, g, shift, scale):
    return _rms(x, g) * (1.0 + scale) + shift


def _causal(n, reverse):
    r = lax.broadcasted_iota(I32, (n, n), 0)
    s = lax.broadcasted_iota(I32, (n, n), 1)
    return (s >= r) if reverse else (s <= r)


def _ada_body(c_ref, w_ref, b_ref, o_ref):
    a = _silu(c_ref[...])
    o_ref[0] = _dot_hi(a, w_ref[0]) + b_ref[0]


def _ada(cstack, ada_w, ada_b):
    depth, d, n6 = ada_w.shape
    nb = cstack.shape[0]
    return pl.pallas_call(
        _ada_body,
        grid=(depth, n6 // ADA_TN),
        in_specs=[
            pl.BlockSpec((nb, d), lambda l, j: (0, 0)),
            pl.BlockSpec((1, d, ADA_TN), lambda l, j: (l, 0, j)),
            pl.BlockSpec((1, 1, ADA_TN), lambda l, j: (l, 0, j)),
        ],
        out_specs=pl.BlockSpec((1, nb, ADA_TN), lambda l, j: (l, 0, j)),
        out_shape=jax.ShapeDtypeStruct((depth, nb, n6), F32),
        compiler_params=_params(("arbitrary", "arbitrary")),
        name="ada",
    )(cstack, ada_w, ada_b.reshape(depth, 1, n6))


def _hg_prep_body(x_ref, mod_ref, g_ref, w_ref, lb_ref, q_o, v_o, kf_o, kb_o, lf_o, lbw_o, g_o, *, layer, d):
    mod = mod_ref[0, 0]
    h = _modulate(x_ref[0], g_ref[...], mod[0:1], mod[1:2]).astype(BF16)

    def lower_bound(direction):
        rows = [lb_ref[direction, r : r + 1, :] for r in range(lb_ref.shape[1])]
        mx = functools.reduce(jnp.maximum, rows)
        es = [jnp.exp(r - mx) for r in rows]
        return sum(es[: layer + 1]) / sum(es)

    def proj(j):
        return _dot(h, w_ref[:, j * d : (j + 1) * d])

    q_o[0] = _silu(proj(0)).astype(BF16)
    v_o[0] = proj(1).astype(BF16)
    for j, k_o, l_o in ((2, kf_o, lf_o), (3, kb_o, lbw_o)):
        lb = lower_bound(j - 2)
        f = lb + (1.0 - lb) * _sigmoid(proj(j))
        k_o[0] = (1.0 - f).astype(BF16)
        l_o[0] = jnp.log(f)
    g_o[0] = proj(4).astype(BF16)


def _hg_prep(xc, mods, norm_g, w_in, lb_logits, layer, nct):
    nb, lc, d = xc.shape
    tile = pl.BlockSpec((1, TM, d), lambda b, i: (b, i, 0))
    out = lambda dt: jax.ShapeDtypeStruct((nb, lc, d), dt)
    return pl.pallas_call(
        functools.partial(_hg_prep_body, layer=layer, d=d),
        grid=(nb, lc // TM),
        in_specs=[
            tile,
            pl.BlockSpec((1, 1, 6, d), lambda b, i: (layer, jnp.where(i < nct, nb, b), 0, 0)),
            pl.BlockSpec((1, d), lambda b, i: (0, 0)),
            pl.BlockSpec((d, 5 * d), lambda b, i: (0, 0)),
            pl.BlockSpec(lb_logits.shape, lambda b, i: (0, 0, 0)),
        ],
        out_specs=[tile] * 7,
        out_shape=[out(BF16), out(BF16), out(BF16), out(BF16), out(F32), out(F32), out(BF16)],
        compiler_params=_params(("arbitrary", "arbitrary"), VMEM_LIMIT),
        name="hg_prep",
    )(xc, mods, norm_g.reshape(1, d), w_in, lb_logits)


def _hg_dir(q_ref, v_ref, k_ref, l_ref, o_ref, s_ref, reverse):
    c = CHUNK
    causal = _causal(c, reverse)
    bc = _dot_exact_lhs(causal.astype(BF16), l_ref[0])
    mid = (c - 1 - c // 2) if reverse else c // 2
    end = 0 if reverse else c - 1
    b_mid = bc[mid : mid + 1]
    b_end = bc[end : end + 1]
    q = q_ref[0].astype(F32)
    k = k_ref[0].astype(F32)
    v = v_ref[0]
    qd = (q * jnp.exp(bc - b_mid)).astype(BF16)
    kd = (k * jnp.exp(b_mid - bc)).astype(BF16)
    qs = (q * jnp.exp(bc)).astype(BF16)
    ke = (k * jnp.exp(b_end - bc)).astype(BF16)
    decay = jnp.exp(b_end)
    dh = q.shape[-1] // HEADS
    for h in range(HEADS):
        sl = slice(h * dh, (h + 1) * dh)
        att = jnp.where(causal, _dot_nt(qd[:, sl], kd[:, sl]), 0.0).astype(BF16)
        st = s_ref[h]
        o_ref[0, :, sl] = _dot(att, v[:, sl]) + _dot_nt(qs[:, sl], st.astype(BF16))
        s_ref[h] = st * decay[:, sl] + _dot_tn(v[:, sl], ke[:, sl])


def _hg_scan_body(qf, vf, kf, lf, qb, vb, kb, lb, of, ob, sf, sb):
    @pl.when(pl.program_id(1) == 0)
    def _():
        sf[...] = jnp.zeros_like(sf)
        sb[...] = jnp.zeros_like(sb)

    _hg_dir(qf, vf, kf, lf, of, sf, False)
    _hg_dir(qb, vb, kb, lb, ob, sb, True)


def _chunk_maps(ncc, nc):
    fwd = lambda b, c: (b, c, 0)
    bwd = lambda b, c: (b, jnp.where(c < ncc, ncc - 1 - c, nc - 1 + ncc - c), 0)
    return fwd, bwd


def _hg_scan(q, v, kf, kb, lf, lb, ncc):
    nb, lc, d = q.shape
    nc = lc // CHUNK
    fwd, bwd = _chunk_maps(ncc, nc)
    blk = lambda m: pl.BlockSpec((1, CHUNK, d), m)
    dh = d // HEADS
    return pl.pallas_call(
        _hg_scan_body,
        grid=(nb, nc),
        in_specs=[blk(fwd)] * 4 + [blk(bwd)] * 4,
        out_specs=[blk(fwd), blk(bwd)],
        out_shape=[jax.ShapeDtypeStruct((nb, lc, d), F32)] * 2,
        scratch_shapes=[pltpu.VMEM((HEADS, dh, dh), F32)] * 2,
        compiler_params=_params(("arbitrary", "arbitrary")),
        name="hg_scan",
    )(q, v, kf, lf, q, v, kb, lb)


def _post_body(of, ob, g_ref, x_ref, mod_ref, ng_ref, wo_ref, nf_ref, rw_ref, rb_ref, x_o, f_o, ti_o, tg_o, *, gate):
    y = of[0] + ob[0]
    dh = y.shape[-1] // HEADS
    parts = []
    for h in range(HEADS):
        yh = y[:, h * dh : (h + 1) * dh]
        parts.append(yh * lax.rsqrt(jnp.mean(yh * yh, axis=-1, keepdims=True) + EPS))
    gt = g_ref[0].astype(F32)
    act = _silu(gt) if gate == "silu" else _sigmoid(gt)
    yn = jnp.concatenate(parts, axis=-1) * ng_ref[...] * act
    mod = mod_ref[0, 0]
    xn = x_ref[0] + mod[2:3] * _dot(yn.astype(BF16), wo_ref[...])
    x_o[0] = xn
    f = _modulate(xn, nf_ref[...], mod[3:4], mod[4:5])
    f_o[0] = f
    vals = _dot_hi(f, rw_ref[...]) + rb_ref[...]
    lane = lax.broadcasted_iota(I32, vals.shape, 1)
    tops, ids = [], []
    for _ in range(TOP_K):
        m = jnp.max(vals, axis=-1, keepdims=True)
        idx = jnp.min(jnp.where(vals == m, lane, N_EXPERTS), axis=-1, keepdims=True)
        tops.append(m)
        ids.append(idx)
        vals = jnp.where(lane == idx, -jnp.inf, vals)
    es = [jnp.exp(m - tops[0]) for m in tops]
    tot = functools.reduce(lambda a, b: a + b, es)
    tg_o[0] = jnp.concatenate([e / tot for e in es], axis=-1)
    ti_o[0] = jnp.concatenate(ids, axis=-1)


def _post(of, ob, g, xc, mods, norm_g, w_out, norm_ffn_g, router_w, router_b, layer, nct, skip, gate):
    nb, lc, d = xc.shape
    nt = lc // TM - skip
    tile = pl.BlockSpec((1, TM, d), lambda b, i: (b, i + skip, 0))
    otile = pl.BlockSpec((1, TM, d), lambda b, i: (b, i, 0))
    small = pl.BlockSpec((1, TM, TOP_K), lambda b, i: (b, i, 0))
    row = pl.BlockSpec((1, d), lambda b, i: (0, 0))
    return pl.pallas_call(
        functools.partial(_post_body, gate=gate),
        grid=(nb, nt),
        in_specs=[
            tile, tile, tile, tile,
            pl.BlockSpec((1, 1, 6, d), lambda b, i: (layer, jnp.where(i + skip < nct, nb, b), 0, 0)),
            row,
            pl.BlockSpec((d, d), lambda b, i: (0, 0)),
            row,
            pl.BlockSpec((d, N_EXPERTS), lambda b, i: (0, 0)),
            pl.BlockSpec((1, N_EXPERTS), lambda b, i: (0, 0)),
        ],
        out_specs=[tile, otile, small, small],
        out_shape=[
            jax.ShapeDtypeStruct((nb, lc, d), F32),
            jax.ShapeDtypeStruct((nb, nt * TM, d), F32),
            jax.ShapeDtypeStruct((nb, nt * TM, TOP_K), I32),
            jax.ShapeDtypeStruct((nb, nt * TM, TOP_K), F32),
        ],
        input_output_aliases={3: 0},
        compiler_params=_params(("arbitrary", "arbitrary")),
        name="post_mixer",
    )(of, ob, g, xc, mods, norm_g.reshape(1, d), w_out, norm_ffn_g.reshape(1, d), router_w,
      router_b.reshape(1, N_EXPERTS))


def _rank_body(ti_ref, rank_o, cnt_o, carry):
    @pl.when(pl.program_id(0) == 0)
    def _():
        carry[...] = jnp.zeros_like(carry)

    ti = ti_ref[...]
    lane = lax.broadcasted_iota(I32, (TM, N_EXPERTS), 1)
    hots = [lane == ti[:, k : k + 1] for k in range(TOP_K)]
    chosen = functools.reduce(lambda a, b: a + b, [h.astype(F32) for h in hots])
    r = lax.broadcasted_iota(I32, (TM, TM), 0)
    s = lax.broadcasted_iota(I32, (TM, TM), 1)
    before = _dot((s < r).astype(BF16), chosen.astype(BF16)) + carry[...]
    ranks = [jnp.sum(jnp.where(h, before, 0.0), axis=-1, keepdims=True) for h in hots]
    rank_o[...] = jnp.concatenate(ranks, axis=-1).astype(I32)
    carry[...] += jnp.sum(chosen, axis=0, keepdims=True)
    cnt_o[...] = carry[...]


def _rank(ti):
    t = ti.shape[0]
    return pl.pallas_call(
        _rank_body,
        grid=(t // TM,),
        in_specs=[pl.BlockSpec((TM, TOP_K), lambda i: (i, 0))],
        out_specs=[pl.BlockSpec((TM, TOP_K), lambda i: (i, 0)), pl.BlockSpec((1, N_EXPERTS), lambda i: (0, 0))],
        out_shape=[jax.ShapeDtypeStruct((t, TOP_K), I32), jax.ShapeDtypeStruct((1, N_EXPERTS), F32)],
        scratch_shapes=[pltpu.VMEM((1, N_EXPERTS), F32)],
        compiler_params=_params(("arbitrary",)),
        name="moe_rank",
    )(ti)


def _row_copy(src, i, dst, j, sem):
    return pltpu.make_async_copy(src.at[pl.ds(i, 1)], dst.at[pl.ds(j, 1)], sem)


def _dispatch_body(ps_ref, cnt_ref, pe_ref, ti_ref, rk_ref, f_ref, xs_hbm, zrow, sem, zsem, *, n_rows, n_pad):
    i = pl.program_id(0)

    def issue(s, carry):
        dst = ps_ref[ti_ref[s]] + rk_ref[s]
        _row_copy(f_ref, lax.shift_right_logical(s, 2), xs_hbm, dst, sem).start()
        return carry

    lax.fori_loop(0, TM * TOP_K, issue, 0, unroll=8)

    @pl.when(i == 0)
    def _():
        zrow[...] = jnp.zeros_like(zrow)

        def zero(r, carry):
            _row_copy(zrow, 0, xs_hbm, r, zsem).start()
            return carry

        def per_expert(e, carry):
            lax.fori_loop(ps_ref[e] + cnt_ref[e], pe_ref[e], zero, 0)
            return carry

        lax.fori_loop(0, N_EXPERTS, per_expert, 0)
        lax.fori_loop(pe_ref[N_EXPERTS - 1], n_rows, zero, 0)

        def wait_zero(r, carry):
            _row_copy(zrow, 0, xs_hbm, 0, zsem).wait()
            return carry

        lax.fori_loop(0, n_pad, wait_zero, 0)

    pltpu.make_async_copy(xs_hbm.at[pl.ds(0, TM * TOP_K)], xs_hbm.at[pl.ds(0, TM * TOP_K)], sem).wait()


def _dispatch(pstarts, counts, pends, ti_flat, rk_flat, f, n_rows):
    t, d = f.shape
    n_slots = TM * TOP_K
    return pl.pallas_call(
        functools.partial(_dispatch_body, n_rows=n_rows, n_pad=n_rows - t * TOP_K),
        grid_spec=pltpu.PrefetchScalarGridSpec(
            num_scalar_prefetch=3,
            grid=(t // TM,),
            in_specs=[
                pl.BlockSpec((n_slots,), lambda i, *_: (i,), memory_space=pltpu.SMEM),
                pl.BlockSpec((n_slots,), lambda i, *_: (i,), memory_space=pltpu.SMEM),
                pl.BlockSpec((TM, d), lambda i, *_: (i, 0)),
            ],
            out_specs=pl.BlockSpec(memory_space=pl.ANY),
            scratch_shapes=[pltpu.VMEM((8, d), F32), pltpu.SemaphoreType.DMA, pltpu.SemaphoreType.DMA],
        ),
        out_shape=jax.ShapeDtypeStruct((n_rows, d), F32),
        compiler_params=_params(("arbitrary",)),
        name="moe_dispatch",
    )(pstarts, counts, pends, ti_flat, rk_flat, f)


def _ffn_body(be_ref, nu_ref, xs_ref, w1_ref, b1_ref, w2_ref, b2_ref, ys_ref, w1b, w2b):
    b = pl.program_id(0)
    changed = jnp.logical_or(b == 0, be_ref[b] != be_ref[jnp.maximum(b - 1, 0)])

    @pl.when(changed)
    def _():
        w1b[...] = w1_ref[0].astype(BF16)
        w2b[...] = w2_ref[0].astype(BF16)

    @pl.when(b < nu_ref[0])
    def _():
        u = _dot(xs_ref[...].astype(BF16), w1b[...]) + b1_ref[0]
        dff = u.shape[-1] // 2
        glu = jnp.minimum(u[:, :dff], SWIGLU_LIMIT)
        lin = jnp.clip(u[:, dff:], -SWIGLU_LIMIT, SWIGLU_LIMIT)
        y = glu * _sigmoid(SWIGLU_ALPHA * glu) * (lin + 1.0)
        ys_ref[...] = _dot(y.astype(BF16), w2b[...]) + b2_ref[0]

    @pl.when(b >= nu_ref[0])
    def _():
        ys_ref[...] = jnp.zeros_like(ys_ref)


def _ffn(block_e, n_used, xs, w1, b1, w2, b2, layer):
    n_rows, d = xs.shape
    depth, ne, _, dff2 = w1.shape
    b1 = b1.reshape(depth, ne, 1, dff2)
    b2 = b2.reshape(depth, ne, 1, d)
    wmap = lambda b, be, nu: (layer, be[b], 0, 0)
    return pl.pallas_call(
        _ffn_body,
        grid_spec=pltpu.PrefetchScalarGridSpec(
            num_scalar_prefetch=2,
            grid=(n_rows // MOE_BLOCK,),
            in_specs=[
                pl.BlockSpec((MOE_BLOCK, d), lambda b, be, nu: (b, 0)),
                pl.BlockSpec((None, 1, d, dff2), wmap),
                pl.BlockSpec((None, 1, 1, dff2), wmap),
                pl.BlockSpec((None, 1, dff2 // 2, d), wmap),
                pl.BlockSpec((None, 1, 1, d), wmap),
            ],
            out_specs=pl.BlockSpec((MOE_BLOCK, d), lambda b, be, nu: (b, 0)),
            scratch_shapes=[pltpu.VMEM((d, dff2), BF16), pltpu.VMEM((dff2 // 2, d), BF16)],
        ),
        out_shape=jax.ShapeDtypeStruct((n_rows, d), F32),
        compiler_params=_params(("arbitrary",), VMEM_LIMIT),
        name="moe_ffn",
    )(block_e, n_used, xs, w1, b1, w2, b2)


def _combine_body(ps_ref, ti_ref, rk_ref, tg_ref, ys_hbm, x_ref, mod_ref, fg_ref, o_ref, buf, sem, *, final):
    def issue(s, carry):
        src = ps_ref[ti_ref[s]] + rk_ref[s]
        row = (s & (TOP_K - 1)) * TM + lax.shift_right_logical(s, 2)
        _row_copy(ys_hbm, src, buf, row, sem).start()
        return carry

    lax.fori_loop(0, TM * TOP_K, issue, 0, unroll=8)
    pltpu.make_async_copy(ys_hbm.at[pl.ds(0, TM * TOP_K)], buf, sem).wait()
    g = tg_ref[0]
    acc = g[:, 0:1] * buf[0:TM]
    for k in range(1, TOP_K):
        acc += g[:, k : k + 1] * buf[k * TM : (k + 1) * TM]
    xn = x_ref[0] + mod_ref[0, 0][5:6] * acc
    o_ref[0] = _rms(xn, fg_ref[...]) if final else xn


def _combine(pstarts, ti_flat, rk_flat, tg, ys, xc, mods, final_g, layer, nct, skip, final):
    nb, lc, d = xc.shape
    nt = lc // TM - skip
    n_slots = TM * TOP_K
    tile = pl.BlockSpec((1, TM, d), lambda b, i, *_: (b, i + skip, 0))
    if final:
        out_spec = pl.BlockSpec((1, TM, d), lambda b, i, *_: (b, i, 0))
        out_shape = jax.ShapeDtypeStruct((nb, nt * TM, d), F32)
        aliases = {}
    else:
        out_spec, out_shape, aliases = tile, jax.ShapeDtypeStruct((nb, lc, d), F32), {5: 0}
    return pl.pallas_call(
        functools.partial(_combine_body, final=final),
        grid_spec=pltpu.PrefetchScalarGridSpec(
            num_scalar_prefetch=1,
            grid=(nb, nt),
            in_specs=[
                pl.BlockSpec((n_slots,), lambda b, i, *_: (b * nt + i,), memory_space=pltpu.SMEM),
                pl.BlockSpec((n_slots,), lambda b, i, *_: (b * nt + i,), memory_space=pltpu.SMEM),
                pl.BlockSpec((1, TM, TOP_K), lambda b, i, *_: (b, i, 0)),
                pl.BlockSpec(memory_space=pl.ANY),
                tile,
                pl.BlockSpec((1, 1, 6, d), lambda b, i, *_: (layer, jnp.where(i + skip < nct, nb, b), 0, 0)),
                pl.BlockSpec((1, d), lambda b, i, *_: (0, 0)),
            ],
            out_specs=out_spec,
            scratch_shapes=[pltpu.VMEM((n_slots, d), F32), pltpu.SemaphoreType.DMA],
        ),
        out_shape=out_shape,
        input_output_aliases=aliases,
        compiler_params=_params(("arbitrary", "arbitrary")),
        name="moe_combine",
    )(pstarts, ti_flat, rk_flat, tg, ys, xc, mods, final_g.reshape(1, d))


def _moe(f, ti, tg, xc, mods, w1, b1, w2, b2, final_g, layer, nct, skip, final):
    nb, ntok, d = f.shape
    t = nb * ntok
    rank, cnt = _rank(ti.reshape(t, TOP_K))
    counts = cnt[0].astype(I32)
    padded = (counts + MOE_BLOCK - 1) // MOE_BLOCK * MOE_BLOCK
    pends = jnp.cumsum(padded)
    pstarts = pends - padded
    n_slots = t * TOP_K
    n_rows = -(-(n_slots + N_EXPERTS * (MOE_BLOCK - 1)) // MOE_BLOCK) * MOE_BLOCK
    n_blocks = n_rows // MOE_BLOCK
    block_start = jnp.arange(n_blocks, dtype=I32) * MOE_BLOCK
    block_e = jnp.minimum(jnp.sum((pends[None, :] <= block_start[:, None]).astype(I32), axis=1), N_EXPERTS - 1)
    n_used = (pends[-1:] // MOE_BLOCK).astype(I32)
    ti_flat = ti.reshape(n_slots)
    rk_flat = rank.reshape(n_slots)
    xs = _dispatch(pstarts, counts, pends, ti_flat, rk_flat, f.reshape(t, d), n_rows)
    ys = _ffn(block_e, n_used, xs, w1, b1, w2, b2, layer)
    return _combine(pstarts, ti_flat, rk_flat, tg, ys, xc, mods, final_g, layer, nct, skip, final)


def _ml_prep_body(x_ref, mod_ref, g_ref, wqk_ref, wv_ref, wo_ref, wg_ref, bg_ref, qk_o, v_o, o_o, gt_o):
    mod = mod_ref[0, 0]
    h = _modulate(x_ref[0], g_ref[...], mod[0:1], mod[1:2])
    hb = h.astype(BF16)
    qk_o[0] = _dot(hb, wqk_ref[...])
    v_o[0] = _dot(hb, wv_ref[...]).astype(BF16)
    o_o[0] = _dot(hb, wo_ref[...]).astype(BF16)
    gates = GATE_CAP * jnp.tanh((_dot_hi(h, wg_ref[...]) + bg_ref[...]) / GATE_CAP)
    ng = gates.shape[-1] // 2
    fg = gates[:, ng:]
    logf = jnp.minimum(fg, 0.0) - jnp.log1p(jnp.exp(-jnp.abs(fg)))
    pad = jnp.zeros((gates.shape[0], gt_o.shape[-1] - 2 * ng), F32)
    gt_o[0] = jnp.concatenate([gates[:, :ng], logf, pad], axis=-1)


def _ml_prep(xc, mods, norm_g, wqk, wv, wo, wg, bg, layer, nct):
    nb, lc, d = xc.shape
    tile = lambda w: pl.BlockSpec((1, TM, w), lambda b, i: (b, i, 0))
    full = lambda a: pl.BlockSpec(a.shape, lambda b, i: (0,) * a.ndim)
    return pl.pallas_call(
        _ml_prep_body,
        grid=(nb, lc // TM),
        in_specs=[
            tile(d),
            pl.BlockSpec((1, 1, 6, d), lambda b, i: (layer, jnp.where(i < nct, nb, b), 0, 0)),
            pl.BlockSpec((1, d), lambda b, i: (0, 0)),
            full(wqk), full(wv), full(wo), full(wg), full(bg),
        ],
        out_specs=[tile(wqk.shape[1]), tile(wv.shape[1]), tile(wo.shape[1]), tile(128)],
        out_shape=[
            jax.ShapeDtypeStruct((nb, lc, wqk.shape[1]), F32),
            jax.ShapeDtypeStruct((nb, lc, wv.shape[1]), BF16),
            jax.ShapeDtypeStruct((nb, lc, wo.shape[1]), BF16),
            jax.ShapeDtypeStruct((nb, lc, 128), F32),
        ],
        compiler_params=_params(("arbitrary", "arbitrary"), VMEM_LIMIT),
        name="ml_prep",
    )(xc, mods, norm_g.reshape(1, d), wqk, wv, wo, wg, bg)


CONV_PAD = 72


def _conv_body(main, prev, nxt, cw_ref, cb_ref, q_o, k_o, sc, *, nct, dk):
    i = pl.program_id(1)
    last = pl.num_programs(1) - 1
    p = CONV_PAD
    w = main.shape[-1]
    sc[0:p - GRID_W] = jnp.zeros((p - GRID_W, w), F32)
    sc[p + TM + GRID_W:] = jnp.zeros((sc.shape[0] - p - TM - GRID_W, w), F32)
    sc[p - GRID_W:p] = jnp.where(i > nct, prev[0], 0.0)
    sc[p + TM:p + TM + GRID_W] = jnp.where(jnp.logical_and(i >= nct, i < last), nxt[0], 0.0)
    sc[p:p + TM] = main[0]
    col = lax.broadcasted_iota(I32, (TM, 1), 0) & (GRID_W - 1)

    def finish(acc):
        y = _silu(acc)
        q_o[0] = (y[:, :w // 2] * (dk ** -0.5)).astype(BF16)
        k_o[0] = y[:, w // 2:].astype(BF16)

    @pl.when(i < nct)
    def _():
        acc = cb_ref[...] + sc[p:p + TM] * cw_ref[4:5]
        acc += sc[p - 1:p - 1 + TM] * cw_ref[3:4] + sc[p + 1:p + 1 + TM] * cw_ref[5:6]
        finish(acc)

    @pl.when(i >= nct)
    def _():
        acc = jnp.broadcast_to(cb_ref[...], (TM, w))
        for dy in range(3):
            for dx in range(3):
                off = p + (dy - 1) * GRID_W + (dx - 1)
                u = sc[off:off + TM]
                if dx == 0:
                    u = jnp.where(col != 0, u, 0.0)
                if dx == 2:
                    u = jnp.where(col != GRID_W - 1, u, 0.0)
                acc += u * cw_ref[dy * 3 + dx:dy * 3 + dx + 1]
        finish(acc)


def _conv(qk, conv_w, conv_b, nct):
    nb, lc, w = qk.shape
    nt = lc // TM
    per = TM // GRID_W
    nh = lc // GRID_W
    return pl.pallas_call(
        functools.partial(_conv_body, nct=nct, dk=w // 2 // HEADS),
        grid=(nb, nt),
        in_specs=[
            pl.BlockSpec((1, TM, w), lambda b, i: (b, i, 0)),
            pl.BlockSpec((1, GRID_W, w), lambda b, i: (b, jnp.maximum(i * per - 1, 0), 0)),
            pl.BlockSpec((1, GRID_W, w), lambda b, i: (b, jnp.minimum(i * per + per, nh - 1), 0)),
            pl.BlockSpec((9, w), lambda b, i: (0, 0)),
            pl.BlockSpec((1, w), lambda b, i: (0, 0)),
        ],
        out_specs=[pl.BlockSpec((1, TM, w // 2), lambda b, i: (b, i, 0))] * 2,
        out_shape=[jax.ShapeDtypeStruct((nb, lc, w // 2), BF16)] * 2,
        scratch_shapes=[pltpu.VMEM((CONV_PAD + TM + GRID_W + 8, w), F32)],
        compiler_params=_params(("arbitrary", "arbitrary")),
        name="ml_conv",
    )(qk, qk, qk, conv_w.reshape(9, w), conv_b.reshape(1, w))


def _ml_dir(q_ref, k_ref, v_ref, gt_ref, o_ref, c_ref, m_ref, reverse):
    c = CHUNK
    causal = _causal(c, reverse)
    gts = gt_ref[0]
    bsum = _dot_exact_lhs(causal.astype(BF16), gts)
    gts_t = gts.T
    bsum_t = bsum.T
    io = HEADS if reverse else 0
    fo = 2 * HEADS + io
    end = 0 if reverse else c - 1
    q = q_ref[0]
    k = k_ref[0]
    v = v_ref[0]
    dk = q.shape[-1] // HEADS
    dv = v.shape[-1] // HEADS
    one_col = (lax.broadcasted_iota(I32, (c, dv), 1) == 0).astype(BF16)
    for h in range(HEADS):
        bc = bsum[:, fo + h : fo + h + 1]
        br = bsum_t[fo + h : fo + h + 1, :]
        ir = gts_t[io + h : io + h + 1, :]
        ic = gts[:, io + h : io + h + 1]
        m0 = m_ref[h][0:1, 0:1]
        dmat = jnp.where(causal, bc - br + ir, -jnp.inf)
        inter = bc + m0
        mt = jnp.maximum(inter, jnp.max(dmat, axis=-1, keepdims=True))
        w_intra = jnp.exp(dmat - mt)
        w_inter = jnp.exp(inter - mt)
        qh = q[:, h * dk : (h + 1) * dk]
        kh = k[:, h * dk : (h + 1) * dk]
        v_aug = jnp.concatenate([v[:, h * dv : (h + 1) * dv], one_col], axis=-1)
        sc = _dot_nt(qh, kh) * w_intra
        lhs = jnp.concatenate([sc, w_inter * qh.astype(F32)], axis=-1).astype(BF16)
        ca = c_ref[h]
        nd = _dot(lhs, jnp.concatenate([v_aug, ca.astype(BF16)], axis=0))
        den = nd[:, dv : dv + 1]
        o_ref[0, :, h * dv : (h + 1) * dv] = nd[:, :dv] / jnp.maximum(jnp.abs(den), jnp.exp(-mt))
        b_end = bc[end : end + 1]
        g = b_end - bc + ic
        m_new = jnp.maximum(b_end + m0, jnp.max(g, axis=0, keepdims=True))
        kw = (jnp.exp(g - m_new) * kh.astype(F32)).astype(BF16)
        c_ref[h] = jnp.exp(b_end + m0 - m_new) * ca + _dot_tn(kw, v_aug)
        m_ref[h] = jnp.broadcast_to(m_new, m_ref.shape[1:])


def _ml_scan_body(qf, kf, vf, gf, qb, kb, vb, gb, of, ob, cf, cb, mf, mb):
    @pl.when(pl.program_id(1) == 0)
    def _():
        for ref in (cf, cb, mf, mb):
            ref[...] = jnp.zeros_like(ref)

    _ml_dir(qf, kf, vf, gf, of, cf, mf, False)
    _ml_dir(qb, kb, vb, gb, ob, cb, mb, True)


def _ml_scan(q, k, v, gates, ncc):
    nb, lc, dqk = q.shape
    dvv = v.shape[-1]
    nc = lc // CHUNK
    fwd, bwd = _chunk_maps(ncc, nc)
    blk = lambda w, m: pl.BlockSpec((1, CHUNK, w), m)
    ins = lambda m: [blk(dqk, m), blk(dqk, m), blk(dvv, m), blk(gates.shape[-1], m)]
    return pl.pallas_call(
        _ml_scan_body,
        grid=(nb, nc),
        in_specs=ins(fwd) + ins(bwd),
        out_specs=[blk(dvv, fwd), blk(dvv, bwd)],
        out_shape=[jax.ShapeDtypeStruct((nb, lc, dvv), F32)] * 2,
        scratch_shapes=[pltpu.VMEM((HEADS, dqk // HEADS, 2 * dvv // HEADS), F32)] * 2
        + [pltpu.VMEM((HEADS, 8, 128), F32)] * 2,
        compiler_params=_params(("arbitrary", "arbitrary")),
        name="ml_scan",
    )(q, k, v, gates, q, k, v, gates)


def kernel(x, c, ctx, c_ctx, ada_w, ada_b, norm_mix_g, norm_ffn_g, hg_w_in, hg_lb_logits, hg_norm_g, hg_w_out,
           ml_w_in, ml_b_gate, ml_conv_w, ml_conv_b, ml_norm_g, ml_w_out, router_w, router_b,
           moe_w1, moe_b1, moe_w2, moe_b2, final_norm_g):
    nb, seq, d = x.shape
    ctx_len = ctx.shape[1]
    depth = ada_w.shape[0]
    assert ctx_len == TM and seq % TM == 0 and seq % GRID_W == 0 and d % (HEADS * 128) == 0
    nct = ctx_len // TM
    ncc = ctx_len // CHUNK

    xc = jnp.concatenate([ctx, x], axis=1)
    rows = -(-(nb + 1) // 8) * 8
    cstack = jnp.concatenate([c, c_ctx[None, :], jnp.zeros((rows - nb - 1, d), F32)], axis=0)
    mods = _ada(cstack, ada_w, ada_b).reshape(depth, rows, 6, d)

    for layer in range(depth):
        j = layer // 2
        last = layer == depth - 1
        if layer % 2 == 0:
            q, v, kf, kb, lf, lb, g = _hg_prep(xc, mods, norm_mix_g[layer], hg_w_in[j].astype(BF16),
                                               hg_lb_logits, layer, nct)
            of, ob = _hg_scan(q, v, kf, kb, lf, lb, ncc)
            norm_g, w_out, gate = hg_norm_g[j], hg_w_out[j], "silu"
        else:
            w = ml_w_in[j]
            dqk = ml_conv_w.shape[-1]
            dv = ml_w_out.shape[1]
            qk, v, g, gates = _ml_prep(
                xc, mods, norm_mix_g[layer], w[:, :dqk].astype(BF16), w[:, dqk:dqk + dv].astype(BF16),
                w[:, dqk + dv:dqk + 2 * dv].astype(BF16), w[:, dqk + 2 * dv:], ml_b_gate[j][None, :], layer, nct)
            q, k = _conv(qk, ml_conv_w[j], ml_conv_b[j], nct)
            of, ob = _ml_scan(q, k, v, gates, ncc)
            norm_g, w_out, gate = ml_norm_g[j], ml_w_out[j], "sigmoid"
        skip = nct if last else 0
        xc, f, ti, tg = _post(of, ob, g, xc, mods, norm_g, w_out.astype(BF16), norm_ffn_g[layer],
                              router_w[layer], router_b[layer], layer, nct, skip, gate)
        xc = _moe(f, ti, tg, xc, mods, moe_w1, moe_b1, moe_w2, moe_b2, final_norm_g, layer, nct, skip, last)
    return xc
```

```python
import functools

import jax
import jax.numpy as jnp
from jax import lax
from jax.experimental import pallas as pl
from jax.experimental.pallas import tpu as pltpu

F32 = jnp.float32
BF16 = jnp.bfloat16
I32 = jnp.int32

EPS = 1e-6
HEADS = 8
CHUNK = 64
GATE_CAP = 15.0
GRID_W = 64
N_EXPERTS = 32
TOP_K = 4
MOE_BLOCK = 256
SWIGLU_LIMIT = 7.0
SWIGLU_ALPHA = 1.702

TM = 256
ADA_TN = 512
VMEM_LIMIT = 56 * 1024 * 1024


def _params(sem, vmem=None):
    return pltpu.CompilerParams(dimension_semantics=sem, vmem_limit_bytes=vmem)


def _dot(a, b):
    return jnp.dot(a, b, preferred_element_type=F32)


def _dot_nt(a, b):
    return lax.dot_general(a, b, (((1,), (1,)), ((), ())), preferred_element_type=F32)


def _dot_tn(a, b):
    return lax.dot_general(a, b, (((0,), (0,)), ((), ())), preferred_element_type=F32)


def _split2(x):
    hi = x.astype(BF16)
    lo = (x - hi.astype(F32)).astype(BF16)
    return hi, lo


def _split3(x):
    p1 = x.astype(BF16)
    r1 = x - p1.astype(F32)
    p2 = r1.astype(BF16)
    p3 = (r1 - p2.astype(F32)).astype(BF16)
    return p1, p2, p3


def _dot_hi(a, w):
    a1, a2 = _split2(a)
    w1, w2 = _split2(w)
    return _dot(a1, w1) + (_dot(a1, w2) + _dot(a2, w1))


def _dot_exact_lhs(t, x):
    p1, p2, p3 = _split3(x)
    return _dot(t, p1) + (_dot(t, p2) + _dot(t, p3))


def _sigmoid(x):
    return jax.nn.sigmoid(x)


def _silu(x):
    return x * jax.nn.sigmoid(x)


def _rms(x, g):
    return x * lax.rsqrt(jnp.mean(x * x, axis=-1, keepdims=True) + EPS) * g


def _modulate(x, g, shift, scale):
    return _rms(x, g) * (1.0 + scale) + shift


def _causal(n, reverse):
    r = lax.broadcasted_iota(I32, (n, n), 0)
    s = lax.broadcasted_iota(I32, (n, n), 1)
    return (s >= r) if reverse else (s <= r)


def _ada_body(c_ref, w_ref, b_ref, o_ref):
    a = _silu(c_ref[...])
    o_ref[0] = _dot_hi(a, w_ref[0]) + b_ref[0]


def _ada(cstack, ada_w, ada_b):
    depth, d, n6 = ada_w.shape
    nb = cstack.shape[0]
    return pl.pallas_call(
        _ada_body,
        grid=(depth, n6 // ADA_TN),
        in_specs=[
            pl.BlockSpec((nb, d), lambda l, j: (0, 0)),
            pl.BlockSpec((1, d, ADA_TN), lambda l, j: (l, 0, j)),
            pl.BlockSpec((1, 1, ADA_TN), lambda l, j: (l, 0, j)),
        ],
        out_specs=pl.BlockSpec((1, nb, ADA_TN), lambda l, j: (l, 0, j)),
        out_shape=jax.ShapeDtypeStruct((depth, nb, n6), F32),
        compiler_params=_params(("arbitrary", "arbitrary")),
        name="ada",
    )(cstack, ada_w, ada_b.reshape(depth, 1, n6))


def _hg_prep_body(x_ref, mod_ref, g_ref, w_ref, lb_ref, q_o, v_o, kf_o, kb_o, lf_o, lbw_o, g_o, *, layer, d):
    mod = mod_ref[0, 0]
    h = _modulate(x_ref[0], g_ref[...], mod[0:1], mod[1:2]).astype(BF16)

    def lower_bound(direction):
        rows = [lb_ref[direction, r : r + 1, :] for r in range(lb_ref.shape[1])]
        mx = functools.reduce(jnp.maximum, rows)
        es = [jnp.exp(r - mx) for r in rows]
        return sum(es[: layer + 1]) / sum(es)

    def proj(j):
        return _dot(h, w_ref[:, j * d : (j + 1) * d])

    q_o[0] = _silu(proj(0)).astype(BF16)
    v_o[0] = proj(1).astype(BF16)
    for j, k_o, l_o in ((2, kf_o, lf_o), (3, kb_o, lbw_o)):
        lb = lower_bound(j - 2)
        f = lb + (1.0 - lb) * _sigmoid(proj(j))
        k_o[0] = (1.0 - f).astype(BF16)
        l_o[0] = jnp.log(f)
    g_o[0] = proj(4).astype(BF16)


def _hg_prep(xc, mods, norm_g, w_in, lb_logits, layer, nct):
    nb, lc, d = xc.shape
    tile = pl.BlockSpec((1, TM, d), lambda b, i: (b, i, 0))
    out = lambda dt: jax.ShapeDtypeStruct((nb, lc, d), dt)
    return pl.pallas_call(
        functools.partial(_hg_prep_body, layer=layer, d=d),
        grid=(nb, lc // TM),
        in_specs=[
            tile,
            pl.BlockSpec((1, 1, 6, d), lambda b, i: (layer, jnp.where(i < nct, nb, b), 0, 0)),
            pl.BlockSpec((1, d), lambda b, i: (0, 0)),
            pl.BlockSpec((d, 5 * d), lambda b, i: (0, 0)),
            pl.BlockSpec(lb_logits.shape, lambda b, i: (0, 0, 0)),
        ],
        out_specs=[tile] * 7,
        out_shape=[out(BF16), out(BF16), out(BF16), out(BF16), out(F32), out(F32), out(BF16)],
        compiler_params=_params(("arbitrary", "arbitrary"), VMEM_LIMIT),
        name="hg_prep",
    )(xc, mods, norm_g.reshape(1, d), w_in, lb_logits)


def _hg_dir(q_ref, v_ref, k_ref, l_ref, o_ref, s_ref, reverse):
    c = CHUNK
    causal = _causal(c, reverse)
    bc = _dot_exact_lhs(causal.astype(BF16), l_ref[0])
    mid = (c - 1 - c // 2) if reverse else c // 2
    end = 0 if reverse else c - 1
    b_mid = bc[mid : mid + 1]
    b_end = bc[end : end + 1]
    q = q_ref[0].astype(F32)
    k = k_ref[0].astype(F32)
    v = v_ref[0]
    qd = (q * jnp.exp(bc - b_mid)).astype(BF16)
    kd = (k * jnp.exp(b_mid - bc)).astype(BF16)
    qs = (q * jnp.exp(bc)).astype(BF16)
    ke = (k * jnp.exp(b_end - bc)).astype(BF16)
    decay = jnp.exp(b_end)
    dh = q.shape[-1] // HEADS
    for h in range(HEADS):
        sl = slice(h * dh, (h + 1) * dh)
        att = jnp.where(causal, _dot_nt(qd[:, sl], kd[:, sl]), 0.0).astype(BF16)
        st = s_ref[h]
        o_ref[0, :, sl] = _dot(att, v[:, sl]) + _dot_nt(qs[:, sl], st.astype(BF16))
        s_ref[h] = st * decay[:, sl] + _dot_tn(v[:, sl], ke[:, sl])


def _hg_scan_body(qf, vf, kf, lf, qb, vb, kb, lb, of, ob, sf, sb):
    @pl.when(pl.program_id(1) == 0)
    def _():
        sf[...] = jnp.zeros_like(sf)
        sb[...] = jnp.zeros_like(sb)

    _hg_dir(qf, vf, kf, lf, of, sf, False)
    _hg_dir(qb, vb, kb, lb, ob, sb, True)


def _chunk_maps(ncc, nc):
    fwd = lambda b, c: (b, c, 0)
    bwd = lambda b, c: (b, jnp.where(c < ncc, ncc - 1 - c, nc - 1 + ncc - c), 0)
    return fwd, bwd


def _hg_scan(q, v, kf, kb, lf, lb, ncc):
    nb, lc, d = q.shape
    nc = lc // CHUNK
    fwd, bwd = _chunk_maps(ncc, nc)
    blk = lambda m: pl.BlockSpec((1, CHUNK, d), m)
    dh = d // HEADS
    return pl.pallas_call(
        _hg_scan_body,
        grid=(nb, nc),
        in_specs=[blk(fwd)] * 4 + [blk(bwd)] * 4,
        out_specs=[blk(fwd), blk(bwd)],
        out_shape=[jax.ShapeDtypeStruct((nb, lc, d), F32)] * 2,
        scratch_shapes=[pltpu.VMEM((HEADS, dh, dh), F32)] * 2,
        compiler_params=_params(("arbitrary", "arbitrary")),
        name="hg_scan",
    )(q, v, kf, lf, q, v, kb, lb)


def _post_body(of, ob, g_ref, x_ref, mod_ref, ng_ref, wo_ref, nf_ref, rw_ref, rb_ref, x_o, f_o, ti_o, tg_o, *, gate):
    y = of[0] + ob[0]
    dh = y.shape[-1] // HEADS
    parts = []
    for h in range(HEADS):
        yh = y[:, h * dh : (h + 1) * dh]
        parts.append(yh * lax.rsqrt(jnp.mean(yh * yh, axis=-1, keepdims=True) + EPS))
    gt = g_ref[0].astype(F32)
    act = _silu(gt) if gate == "silu" else _sigmoid(gt)
    yn = jnp.concatenate(parts, axis=-1) * ng_ref[...] * act
    mod = mod_ref[0, 0]
    xn = x_ref[0] + mod[2:3] * _dot(yn.astype(BF16), wo_ref[...])
    x_o[0] = xn
    f = _modulate(xn, nf_ref[...], mod[3:4], mod[4:5])
    f_o[0] = f
    vals = _dot_hi(f, rw_ref[...]) + rb_ref[...]
    lane = lax.broadcasted_iota(I32, vals.shape, 1)
    tops, ids = [], []
    for _ in range(TOP_K):
        m = jnp.max(vals, axis=-1, keepdims=True)
        idx = jnp.min(jnp.where(vals == m, lane, N_EXPERTS), axis=-1, keepdims=True)
        tops.append(m)
        ids.append(idx)
        vals = jnp.where(lane == idx, -jnp.inf, vals)
    es = [jnp.exp(m - tops[0]) for m in tops]
    tot = functools.reduce(lambda a, b: a + b, es)
    tg_o[0] = jnp.concatenate([e / tot for e in es], axis=-1)
    ti_o[0] = jnp.concatenate(ids, axis=-1)


def _post(of, ob, g, xc, mods, norm_g, w_out, norm_ffn_g, router_w, router_b, layer, nct, skip, gate):
    nb, lc, d = xc.shape
    nt = lc // TM - skip
    tile = pl.BlockSpec((1, TM, d), lambda b, i: (b, i + skip, 0))
    otile = pl.BlockSpec((1, TM, d), lambda b, i: (b, i, 0))
    small = pl.BlockSpec((1, TM, TOP_K), lambda b, i: (b, i, 0))
    row = pl.BlockSpec((1, d), lambda b, i: (0, 0))
    return pl.pallas_call(
        functools.partial(_post_body, gate=gate),
        grid=(nb, nt),
        in_specs=[
            tile, tile, tile, tile,
            pl.BlockSpec((1, 1, 6, d), lambda b, i: (layer, jnp.where(i + skip < nct, nb, b), 0, 0)),
            row,
            pl.BlockSpec((d, d), lambda b, i: (0, 0)),
            row,
            pl.BlockSpec((d, N_EXPERTS), lambda b, i: (0, 0)),
            pl.BlockSpec((1, N_EXPERTS), lambda b, i: (0, 0)),
        ],
        out_specs=[tile, otile, small, small],
        out_shape=[
            jax.ShapeDtypeStruct((nb, lc, d), F32),
            jax.ShapeDtypeStruct((nb, nt * TM, d), F32),
            jax.ShapeDtypeStruct((nb, nt * TM, TOP_K), I32),
            jax.ShapeDtypeStruct((nb, nt * TM, TOP_K), F32),
        ],
        input_output_aliases={3: 0},
        compiler_params=_params(("arbitrary", "arbitrary")),
        name="post_mixer",
    )(of, ob, g, xc, mods, norm_g.reshape(1, d), w_out, norm_ffn_g.reshape(1, d), router_w,
      router_b.reshape(1, N_EXPERTS))


def _rank_body(ti_ref, dest_o, cnt_o, pend_o, carry, pstart):
    phase = pl.program_id(0)
    first = pl.program_id(1) == 0
    ti = ti_ref[...]
    lane = lax.broadcasted_iota(I32, (TM, N_EXPERTS), 1)
    hots = [lane == ti[:, k : k + 1] for k in range(TOP_K)]
    chosen = functools.reduce(lambda a, b: a + b, [h.astype(F32) for h in hots])
    tile_counts = jnp.sum(chosen, axis=0, keepdims=True)

    @pl.when(jnp.logical_and(phase == 0, first))
    def _():
        carry[...] = jnp.zeros_like(carry)

    @pl.when(phase == 0)
    def _():
        carry[...] += tile_counts

    @pl.when(jnp.logical_and(phase == 1, first))
    def _():
        cnt = carry[...]
        padded = jnp.ceil(cnt * (1.0 / MOE_BLOCK)) * MOE_BLOCK
        r = lax.broadcasted_iota(I32, (N_EXPERTS, N_EXPERTS), 0)
        s = lax.broadcasted_iota(I32, (N_EXPERTS, N_EXPERTS), 1)
        upper = (r <= s).astype(BF16)
        p1, p2, p3 = _split3(jnp.broadcast_to(padded, (8, N_EXPERTS)))
        pends = (_dot(p1, upper) + (_dot(p2, upper) + _dot(p3, upper)))[0:1]
        cnt_o[...] = cnt
        pend_o[...] = pends
        pstart[...] = pends - padded
        carry[...] = jnp.zeros_like(carry)

    @pl.when(phase == 1)
    def _():
        r = lax.broadcasted_iota(I32, (TM, TM), 0)
        s = lax.broadcasted_iota(I32, (TM, TM), 1)
        row = _dot((s < r).astype(BF16), chosen.astype(BF16)) + (carry[...] + pstart[...])
        dest = [jnp.sum(jnp.where(h, row, 0.0), axis=-1, keepdims=True) for h in hots]
        dest_o[...] = jnp.concatenate(dest, axis=-1).astype(I32)
        carry[...] += tile_counts


def _rank(ti):
    t = ti.shape[0]
    small = pl.BlockSpec((1, N_EXPERTS), lambda p, i: (0, 0))
    return pl.pallas_call(
        _rank_body,
        grid=(2, t // TM),
        in_specs=[pl.BlockSpec((TM, TOP_K), lambda p, i: (i, 0))],
        out_specs=[pl.BlockSpec((TM, TOP_K), lambda p, i: (i * p, 0)), small, small],
        out_shape=[jax.ShapeDtypeStruct((t, TOP_K), I32)] + [jax.ShapeDtypeStruct((1, N_EXPERTS), F32)] * 2,
        scratch_shapes=[pltpu.VMEM((1, N_EXPERTS), F32)] * 2,
        compiler_params=_params(("arbitrary", "arbitrary")),
        name="moe_rank",
    )(ti)


def _row_copy(src, i, dst, j, sem):
    return pltpu.make_async_copy(src.at[pl.ds(i, 1)], dst.at[pl.ds(j, 1)], sem)


def _dispatch_body(ps_ref, cnt_ref, pe_ref, dest_ref, f_ref, xs_hbm, zrow, sem, zsem, *, n_rows, n_pad):
    i = pl.program_id(0)

    for t in range(TM):
        for k in range(TOP_K):
            _row_copy(f_ref, t, xs_hbm, dest_ref[t * TOP_K + k], sem).start()

    @pl.when(i == 0)
    def _():
        zrow[...] = jnp.zeros_like(zrow)

        def zero(r, carry):
            _row_copy(zrow, 0, xs_hbm, r, zsem).start()
            return carry

        def per_expert(e, carry):
            lax.fori_loop(ps_ref[e] + cnt_ref[e], pe_ref[e], zero, 0)
            return carry

        lax.fori_loop(0, N_EXPERTS, per_expert, 0)
        lax.fori_loop(pe_ref[N_EXPERTS - 1], n_rows, zero, 0)

        def wait_zero(r, carry):
            _row_copy(zrow, 0, xs_hbm, 0, zsem).wait()
            return carry

        lax.fori_loop(0, n_pad, wait_zero, 0)

    pltpu.make_async_copy(xs_hbm.at[pl.ds(0, TM * TOP_K)], xs_hbm.at[pl.ds(0, TM * TOP_K)], sem).wait()


def _dispatch(pstarts, counts, pends, dest_flat, f, n_rows):
    t, d = f.shape
    n_slots = TM * TOP_K
    return pl.pallas_call(
        functools.partial(_dispatch_body, n_rows=n_rows, n_pad=n_rows - t * TOP_K),
        grid_spec=pltpu.PrefetchScalarGridSpec(
            num_scalar_prefetch=3,
            grid=(t // TM,),
            in_specs=[
                pl.BlockSpec((n_slots,), lambda i, *_: (i,), memory_space=pltpu.SMEM),
                pl.BlockSpec((TM, d), lambda i, *_: (i, 0)),
            ],
            out_specs=pl.BlockSpec(memory_space=pl.ANY),
            scratch_shapes=[pltpu.VMEM((8, d), F32), pltpu.SemaphoreType.DMA, pltpu.SemaphoreType.DMA],
        ),
        out_shape=jax.ShapeDtypeStruct((n_rows, d), F32),
        compiler_params=_params(("arbitrary",)),
        name="moe_dispatch",
    )(pstarts, counts, pends, dest_flat, f)


def _ffn_body(be_ref, nu_ref, xs_ref, w1_ref, b1_ref, w2_ref, b2_ref, ys_ref, w1b, w2b):
    b = pl.program_id(0)
    changed = jnp.logical_or(b == 0, be_ref[b] != be_ref[jnp.maximum(b - 1, 0)])

    @pl.when(changed)
    def _():
        w1b[...] = w1_ref[0].astype(BF16)
        w2b[...] = w2_ref[0].astype(BF16)

    @pl.when(b < nu_ref[0])
    def _():
        u = _dot(xs_ref[...].astype(BF16), w1b[...]) + b1_ref[0]
        dff = u.shape[-1] // 2
        glu = jnp.minimum(u[:, :dff], SWIGLU_LIMIT)
        lin = jnp.clip(u[:, dff:], -SWIGLU_LIMIT, SWIGLU_LIMIT)
        y = glu * _sigmoid(SWIGLU_ALPHA * glu) * (lin + 1.0)
        ys_ref[...] = _dot(y.astype(BF16), w2b[...]) + b2_ref[0]

    @pl.when(b >= nu_ref[0])
    def _():
        ys_ref[...] = jnp.zeros_like(ys_ref)


def _ffn(block_e, n_used, xs, w1, b1, w2, b2, layer):
    n_rows, d = xs.shape
    depth, ne, _, dff2 = w1.shape
    b1 = b1.reshape(depth, ne, 1, dff2)
    b2 = b2.reshape(depth, ne, 1, d)
    wmap = lambda b, be, nu: (layer, be[b], 0, 0)
    return pl.pallas_call(
        _ffn_body,
        grid_spec=pltpu.PrefetchScalarGridSpec(
            num_scalar_prefetch=2,
            grid=(n_rows // MOE_BLOCK,),
            in_specs=[
                pl.BlockSpec((MOE_BLOCK, d), lambda b, be, nu: (b, 0)),
                pl.BlockSpec((None, 1, d, dff2), wmap),
                pl.BlockSpec((None, 1, 1, dff2), wmap),
                pl.BlockSpec((None, 1, dff2 // 2, d), wmap),
                pl.BlockSpec((None, 1, 1, d), wmap),
            ],
            out_specs=pl.BlockSpec((MOE_BLOCK, d), lambda b, be, nu: (b, 0)),
            scratch_shapes=[pltpu.VMEM((d, dff2), BF16), pltpu.VMEM((dff2 // 2, d), BF16)],
        ),
        out_shape=jax.ShapeDtypeStruct((n_rows, d), F32),
        compiler_params=_params(("arbitrary",), VMEM_LIMIT),
        name="moe_ffn",
    )(block_e, n_used, xs, w1, b1, w2, b2)


def _combine_body(dest_ref, tg_ref, ys_hbm, x_ref, mod_ref, fg_ref, o_ref, buf, sem, *, final):
    for t in range(TM):
        for k in range(TOP_K):
            _row_copy(ys_hbm, dest_ref[t * TOP_K + k], buf, k * TM + t, sem).start()
    pltpu.make_async_copy(ys_hbm.at[pl.ds(0, TM * TOP_K)], buf, sem).wait()
    g = tg_ref[0]
    acc = g[:, 0:1] * buf[0:TM]
    for k in range(1, TOP_K):
        acc += g[:, k : k + 1] * buf[k * TM : (k + 1) * TM]
    xn = x_ref[0] + mod_ref[0, 0][5:6] * acc
    o_ref[0] = _rms(xn, fg_ref[...]) if final else xn


def _combine(dest_flat, tg, ys, xc, mods, final_g, layer, nct, skip, final):
    nb, lc, d = xc.shape
    nt = lc // TM - skip
    n_slots = TM * TOP_K
    tile = pl.BlockSpec((1, TM, d), lambda b, i: (b, i + skip, 0))
    if final:
        out_spec = pl.BlockSpec((1, TM, d), lambda b, i: (b, i, 0))
        out_shape = jax.ShapeDtypeStruct((nb, nt * TM, d), F32)
        aliases = {}
    else:
        out_spec, out_shape, aliases = tile, jax.ShapeDtypeStruct((nb, lc, d), F32), {3: 0}
    return pl.pallas_call(
        functools.partial(_combine_body, final=final),
        grid=(nb, nt),
        in_specs=[
            pl.BlockSpec((n_slots,), lambda b, i: (b * nt + i,), memory_space=pltpu.SMEM),
            pl.BlockSpec((1, TM, TOP_K), lambda b, i: (b, i, 0)),
            pl.BlockSpec(memory_space=pl.ANY),
            tile,
            pl.BlockSpec((1, 1, 6, d), lambda b, i: (layer, jnp.where(i + skip < nct, nb, b), 0, 0)),
            pl.BlockSpec((1, d), lambda b, i: (0, 0)),
        ],
        out_specs=out_spec,
        scratch_shapes=[pltpu.VMEM((n_slots, d), F32), pltpu.SemaphoreType.DMA],
        out_shape=out_shape,
        input_output_aliases=aliases,
        compiler_params=_params(("arbitrary", "arbitrary")),
        name="moe_combine",
    )(dest_flat, tg, ys, xc, mods, final_g.reshape(1, d))


def _moe(f, ti, tg, xc, mods, w1, b1, w2, b2, final_g, layer, nct, skip, final):
    nb, ntok, d = f.shape
    t = nb * ntok
    dest, cnt, pend = _rank(ti.reshape(t, TOP_K))
    counts = cnt[0].astype(I32)
    pends = pend[0].astype(I32)
    pstarts = pends - (counts + MOE_BLOCK - 1) // MOE_BLOCK * MOE_BLOCK
    n_slots = t * TOP_K
    n_rows = -(-(n_slots + N_EXPERTS * (MOE_BLOCK - 1)) // MOE_BLOCK) * MOE_BLOCK
    n_blocks = n_rows // MOE_BLOCK
    block_start = jnp.arange(n_blocks, dtype=I32) * MOE_BLOCK
    block_e = jnp.minimum(jnp.sum((pends[None, :] <= block_start[:, None]).astype(I32), axis=1), N_EXPERTS - 1)
    n_used = (pends[-1:] // MOE_BLOCK).astype(I32)
    dest_flat = dest.reshape(n_slots)
    xs = _dispatch(pstarts, counts, pends, dest_flat, f.reshape(t, d), n_rows)
    ys = _ffn(block_e, n_used, xs, w1, b1, w2, b2, layer)
    return _combine(dest_flat, tg, ys, xc, mods, final_g, layer, nct, skip, final)


def _ml_prep_body(x_ref, mod_ref, g_ref, wqk_ref, wv_ref, wo_ref, wg_ref, bg_ref,
                  qk_o, v_o, o_o, gi_o, gf_o, git_o, gft_o):
    mod = mod_ref[0, 0]
    h = _modulate(x_ref[0], g_ref[...], mod[0:1], mod[1:2])
    hb = h.astype(BF16)
    qk_o[0] = _dot(hb, wqk_ref[...])
    v_o[0] = _dot(hb, wv_ref[...]).astype(BF16)
    o_o[0] = _dot(hb, wo_ref[...]).astype(BF16)
    gates = GATE_CAP * jnp.tanh((_dot_hi(h, wg_ref[...]) + bg_ref[...]) / GATE_CAP)
    ng = gates.shape[-1] // 2
    fg = gates[:, ng:]
    logf = jnp.minimum(fg, 0.0) - jnp.log1p(jnp.exp(-jnp.abs(fg)))
    pad = jnp.zeros((gates.shape[0], gi_o.shape[-1] - ng), F32)
    gi = jnp.concatenate([gates[:, :ng], pad], axis=-1)
    gf = jnp.concatenate([logf, pad], axis=-1)
    gi_o[0] = gi
    gf_o[0] = gf
    git_o[0] = gi.T[:ng]
    gft_o[0] = gf.T[:ng]


def _ml_prep(xc, mods, norm_g, wqk, wv, wo, wg, bg, layer, nct):
    nb, lc, d = xc.shape
    ng = wg.shape[1] // 2
    tile = lambda w: pl.BlockSpec((1, TM, w), lambda b, i: (b, i, 0))
    full = lambda a: pl.BlockSpec(a.shape, lambda b, i: (0,) * a.ndim)
    return pl.pallas_call(
        _ml_prep_body,
        grid=(nb, lc // TM),
        in_specs=[
            tile(d),
            pl.BlockSpec((1, 1, 6, d), lambda b, i: (layer, jnp.where(i < nct, nb, b), 0, 0)),
            pl.BlockSpec((1, d), lambda b, i: (0, 0)),
            full(wqk), full(wv), full(wo), full(wg), full(bg),
        ],
        out_specs=[tile(wqk.shape[1]), tile(wv.shape[1]), tile(wo.shape[1]), tile(128), tile(128),
                   pl.BlockSpec((1, ng, TM), lambda b, i: (b, 0, i)), pl.BlockSpec((1, ng, TM), lambda b, i: (b, 0, i))],
        out_shape=[
            jax.ShapeDtypeStruct((nb, lc, wqk.shape[1]), F32),
            jax.ShapeDtypeStruct((nb, lc, wv.shape[1]), BF16),
            jax.ShapeDtypeStruct((nb, lc, wo.shape[1]), BF16),
            jax.ShapeDtypeStruct((nb, lc, 128), F32),
            jax.ShapeDtypeStruct((nb, lc, 128), F32),
            jax.ShapeDtypeStruct((nb, ng, lc), F32),
            jax.ShapeDtypeStruct((nb, ng, lc), F32),
        ],
        compiler_params=_params(("arbitrary", "arbitrary"), VMEM_LIMIT),
        name="ml_prep",
    )(xc, mods, norm_g.reshape(1, d), wqk, wv, wo, wg, bg)


CONV_PAD = 72


def _conv_body(main, prev, nxt, cw_ref, cb_ref, q_o, k_o, sc, *, nct, dk):
    i = pl.program_id(1)
    last = pl.num_programs(1) - 1
    p = CONV_PAD
    w = main.shape[-1]
    sc[0:p - GRID_W] = jnp.zeros((p - GRID_W, w), F32)
    sc[p + TM + GRID_W:] = jnp.zeros((sc.shape[0] - p - TM - GRID_W, w), F32)
    sc[p - GRID_W:p] = jnp.where(i > nct, prev[0], 0.0)
    sc[p + TM:p + TM + GRID_W] = jnp.where(jnp.logical_and(i >= nct, i < last), nxt[0], 0.0)
    sc[p:p + TM] = main[0]
    col = lax.broadcasted_iota(I32, (TM, 1), 0) & (GRID_W - 1)

    def finish(acc):
        y = _silu(acc)
        kt = y[:, w // 2:].T
        qs, ks = [], []
        for h in range(HEADS):
            qs += [y[:, h * dk:(h + 1) * dk] * (dk ** -0.5), jnp.zeros((TM, 128 - dk), F32)]
            ks += [kt[h * dk:(h + 1) * dk], jnp.zeros((128 - dk, TM), F32)]
        q_o[0] = jnp.concatenate(qs, axis=1).astype(BF16)
        k_o[0] = jnp.concatenate(ks, axis=0).astype(BF16)

    @pl.when(i < nct)
    def _():
        acc = cb_ref[...] + sc[p:p + TM] * cw_ref[4:5]
        acc += sc[p - 1:p - 1 + TM] * cw_ref[3:4] + sc[p + 1:p + 1 + TM] * cw_ref[5:6]
        finish(acc)

    @pl.when(i >= nct)
    def _():
        acc = jnp.broadcast_to(cb_ref[...], (TM, w))
        for dy in range(3):
            for dx in range(3):
                off = p + (dy - 1) * GRID_W + (dx - 1)
                u = sc[off:off + TM]
                if dx == 0:
                    u = jnp.where(col != 0, u, 0.0)
                if dx == 2:
                    u = jnp.where(col != GRID_W - 1, u, 0.0)
                acc += u * cw_ref[dy * 3 + dx:dy * 3 + dx + 1]
        finish(acc)


def _conv(qk, conv_w, conv_b, nct):
    nb, lc, w = qk.shape
    nt = lc // TM
    per = TM // GRID_W
    nh = lc // GRID_W
    return pl.pallas_call(
        functools.partial(_conv_body, nct=nct, dk=w // 2 // HEADS),
        grid=(nb, nt),
        in_specs=[
            pl.BlockSpec((1, TM, w), lambda b, i: (b, i, 0)),
            pl.BlockSpec((1, GRID_W, w), lambda b, i: (b, jnp.maximum(i * per - 1, 0), 0)),
            pl.BlockSpec((1, GRID_W, w), lambda b, i: (b, jnp.minimum(i * per + per, nh - 1), 0)),
            pl.BlockSpec((9, w), lambda b, i: (0, 0)),
            pl.BlockSpec((1, w), lambda b, i: (0, 0)),
        ],
        out_specs=[pl.BlockSpec((1, TM, HEADS * 128), lambda b, i: (b, i, 0)),
                   pl.BlockSpec((1, HEADS * 128, TM), lambda b, i: (b, 0, i))],
        out_shape=[jax.ShapeDtypeStruct((nb, lc, HEADS * 128), BF16),
                   jax.ShapeDtypeStruct((nb, HEADS * 128, lc), BF16)],
        scratch_shapes=[pltpu.VMEM((CONV_PAD + TM + GRID_W + 8, w), F32)],
        compiler_params=_params(("arbitrary", "arbitrary")),
        name="ml_conv",
    )(qk, qk, qk, conv_w.reshape(9, w), conv_b.reshape(1, w))


ML_CHUNK = 128


def _dot_exact_rhs(x, t):
    p1, p2, p3 = _split3(x)
    return _dot(p1, t) + (_dot(p2, t) + _dot(p3, t))


def _ml_scan_body(qf, ktf, vf, gif, gff, gitf, gftf, qb, ktb, vb, gib, gfb, gitb, gftb,
                  of, ob, c_ref, mrow_ref, mcol_ref):
    c = ML_CHUNK
    nd = 2 * HEADS

    @pl.when(pl.program_id(1) == 0)
    def _():
        for ref in (c_ref, mrow_ref, mcol_ref):
            ref[...] = jnp.zeros_like(ref)

    low = _causal(c, False)
    upp = _causal(c, True)
    low_b, upp_b = low.astype(BF16), upp.astype(BF16)
    neg = -jnp.inf

    lane = lax.broadcasted_iota(I32, (c, 128), 1)
    tpos = lax.broadcasted_iota(I32, (c, 128), 0)
    fwd_l = lane < HEADS
    gi = jnp.where(fwd_l, gif[0], gib[0])
    gf = jnp.where(fwd_l, gff[0], gfb[0])
    bs = jnp.where(fwd_l, _dot_exact_lhs(low_b, gf), _dot_exact_lhs(upp_b, gf))
    a = gi - bs
    pre, suf = a, a
    step = 1
    while step < c:
        pre = jnp.maximum(pre, jnp.where(tpos >= step, pltpu.roll(pre, step, 0), neg))
        suf = jnp.maximum(suf, jnp.where(tpos < c - step, pltpu.roll(suf, c - step, 0), neg))
        step *= 2
    m0 = mrow_ref[0:1]
    mm = jnp.maximum(m0, jnp.where(fwd_l, pre, suf))
    w_inter = jnp.exp(m0 - mm)
    floor = jnp.exp(-(bs + mm))
    fwd_r = fwd_l[0:1]
    b_end = jnp.where(fwd_r, bs[c - 1:c], bs[0:1])
    m_end = jnp.where(fwd_r, mm[c - 1:c], mm[0:1])
    decay = jnp.exp(m0 - m_end)
    mrow_ref[...] = jnp.broadcast_to(b_end + m_end, mrow_ref.shape)

    rowj = lax.broadcasted_iota(I32, (nd, c), 0)
    fwd_t = rowj < HEADS
    git = jnp.where(fwd_t, gitf[0], gitb[0])
    gft = jnp.where(fwd_t, gftf[0], gftb[0])
    bst = jnp.where(fwd_t, _dot_exact_rhs(gft, upp_b), _dot_exact_rhs(gft, low_b))
    at = git - bst
    m0c = mcol_ref[...]
    m_end_c = jnp.maximum(m0c[:, 0:1], jnp.max(at, axis=-1, keepdims=True))
    wk_t = jnp.exp(at - m_end_c)
    b_end_c = jnp.where(fwd_t[:, 0:1], bst[:, c - 1:c], bst[:, 0:1])
    mcol_ref[...] = jnp.broadcast_to(b_end_c + m_end_c, mcol_ref.shape)

    wide = HEADS * 128
    head_of_lane = lax.shift_right_logical(lax.broadcasted_iota(I32, (nd, wide), 1), 7)
    row_w = lax.broadcasted_iota(I32, (nd, wide), 0)
    spos = lax.broadcasted_iota(I32, (c, wide), 1) & 127
    tpos_w = lax.broadcasted_iota(I32, (c, wide), 0)
    pad_rows = jnp.zeros((128 - nd, wide), BF16)
    at_parts = _split3(at)
    mm_parts = _split3(mm)
    ones_blk = jnp.ones((c, 128), BF16)
    dirs = ((0, qf, ktf, vf, of, spos <= tpos_w), (1, qb, ktb, vb, ob, spos >= tpos_w))
    for d, q_ref, kt_ref, v_ref, o_ref, causal_w in dirs:
        sel = row_w == HEADS * d + head_of_lane
        sel_b = jnp.concatenate([sel.astype(BF16), pad_rows], axis=0)
        tile8 = lambda x: jnp.concatenate([x] * HEADS, axis=1)
        a_rows = [jnp.where(sel, tile8(p), 0).astype(BF16) for p in at_parts]
        zero_rows = jnp.zeros((128 - 3 * nd, wide), BF16)
        rhs = jnp.concatenate(a_rows + [zero_rows, sel_b, sel_b, sel_b], axis=0)
        lhs = jnp.concatenate(
            [(lane < 3 * nd).astype(BF16)] + [(-p).astype(BF16) for p in mm_parts], axis=1)
        w_intra = jnp.where(causal_w, jnp.exp(_dot(lhs, rhs)), 0.0)
        w_inter_w = _dot(w_inter.astype(BF16), sel_b)
        fl_hi, fl_lo = _split2(floor)
        floor_w = _dot(fl_hi, sel_b) + _dot(fl_lo, sel_b)
        dc_hi, dc_lo = _split2(jnp.broadcast_to(decay, (8, 128)))
        decay_w = (_dot(dc_hi, sel_b) + _dot(dc_lo, sel_b))[0:1]
        q = q_ref[0]
        kt = kt_ref[0]
        v = v_ref[0]
        wq = (w_inter_w * q.astype(F32)).astype(BF16)
        for h in range(HEADS):
            sl = slice(h * 128, (h + 1) * 128)
            j = HEADS * d + h
            sc = (_dot(q[:, sl], kt[sl, :]) * w_intra[:, sl]).astype(BF16)
            v_aug = jnp.concatenate([v[:, sl], ones_blk], axis=1)
            st = c_ref[d, h]
            out = _dot(jnp.concatenate([sc, wq[:, sl]], axis=1),
                       jnp.concatenate([v_aug, st.astype(BF16)], axis=0))
            o_ref[0, :, sl] = out[:, :128] / jnp.maximum(jnp.abs(out[:, 128:]), floor_w[:, sl])
            kw_t = (kt[sl, :].astype(F32) * wk_t[j:j + 1, :]).astype(BF16)
            dec = jnp.concatenate([decay_w[:, sl], decay_w[:, sl]], axis=1)
            c_ref[d, h] = dec * st + _dot(kw_t, v_aug)


def _ml_scan(q, kt, v, gi, gf, git, gft, ctx_len):
    nb, lc, wide = q.shape
    dvv = v.shape[-1]
    nc = lc // ML_CHUNK
    fwd, bwd = _chunk_maps(ctx_len // ML_CHUNK, nc)
    swap = lambda m: (lambda b, c: (m(b, c)[0], 0, m(b, c)[1]))
    tm = lambda w, m: pl.BlockSpec((1, ML_CHUNK, w), m)
    ft = lambda r, m: pl.BlockSpec((1, r, ML_CHUNK), swap(m))
    ng = git.shape[1]
    ins = lambda m: [tm(wide, m), ft(wide, m), tm(dvv, m), tm(128, m), tm(128, m), ft(ng, m), ft(ng, m)]
    return pl.pallas_call(
        _ml_scan_body,
        grid=(nb, nc),
        in_specs=ins(fwd) + ins(bwd),
        out_specs=[tm(dvv, fwd), tm(dvv, bwd)],
        out_shape=[jax.ShapeDtypeStruct((nb, lc, dvv), F32)] * 2,
        scratch_shapes=[pltpu.VMEM((2, HEADS, 128, 256), F32), pltpu.VMEM((8, 128), F32),
                        pltpu.VMEM((ng, 128), F32)],
        compiler_params=_params(("arbitrary", "arbitrary"), VMEM_LIMIT),
        name="ml_scan",
    )(q, kt, v, gi, gf, git, gft, q, kt, v, gi, gf, git, gft)


def kernel(x, c, ctx, c_ctx, ada_w, ada_b, norm_mix_g, norm_ffn_g, hg_w_in, hg_lb_logits, hg_norm_g, hg_w_out,
           ml_w_in, ml_b_gate, ml_conv_w, ml_conv_b, ml_norm_g, ml_w_out, router_w, router_b,
           moe_w1, moe_b1, moe_w2, moe_b2, final_norm_g):
    nb, seq, d = x.shape
    ctx_len = ctx.shape[1]
    depth = ada_w.shape[0]
    assert ctx_len == TM and seq % TM == 0 and seq % GRID_W == 0 and d % (HEADS * 128) == 0
    nct = ctx_len // TM
    ncc = ctx_len // CHUNK

    xc = jnp.concatenate([ctx, x], axis=1)
    rows = -(-(nb + 1) // 8) * 8
    cstack = jnp.concatenate([c, c_ctx[None, :], jnp.zeros((rows - nb - 1, d), F32)], axis=0)
    mods = _ada(cstack, ada_w, ada_b).reshape(depth, rows, 6, d)

    for layer in range(depth):
        j = layer // 2
        last = layer == depth - 1
        if layer % 2 == 0:
            q, v, kf, kb, lf, lb, g = _hg_prep(xc, mods, norm_mix_g[layer], hg_w_in[j].astype(BF16),
                                               hg_lb_logits, layer, nct)
            of, ob = _hg_scan(q, v, kf, kb, lf, lb, ncc)
            norm_g, w_out, gate = hg_norm_g[j], hg_w_out[j], "silu"
        else:
            w = ml_w_in[j]
            dqk = ml_conv_w.shape[-1]
            dv = ml_w_out.shape[1]
            qk, v, g, gi, gf, git, gft = _ml_prep(
                xc, mods, norm_mix_g[layer], w[:, :dqk].astype(BF16), w[:, dqk:dqk + dv].astype(BF16),
                w[:, dqk + dv:dqk + 2 * dv].astype(BF16), w[:, dqk + 2 * dv:], ml_b_gate[j][None, :], layer, nct)
            q, kt = _conv(qk, ml_conv_w[j], ml_conv_b[j], nct)
            of, ob = _ml_scan(q, kt, v, gi, gf, git, gft, ctx_len)
            norm_g, w_out, gate = ml_norm_g[j], ml_w_out[j], "sigmoid"
        skip = nct if last else 0
        xc, f, ti, tg = _post(of, ob, g, xc, mods, norm_g, w_out.astype(BF16), norm_ffn_g[layer],
                              router_w[layer], router_b[layer], layer, nct, skip, gate)
        xc = _moe(f, ti, tg, xc, mods, moe_w1, moe_b1, moe_w2, moe_b2, final_norm_g, layer, nct, skip, last)
    return xc
```

```python
import functools

import jax
import jax.numpy as jnp
from jax import lax
from jax.experimental import pallas as pl
from jax.experimental.pallas import tpu as pltpu

F32 = jnp.float32
BF16 = jnp.bfloat16
I32 = jnp.int32

EPS = 1e-6
HEADS = 8
CHUNK = 64
GATE_CAP = 15.0
GRID_W = 64
N_EXPERTS = 32
TOP_K = 4
MOE_BLOCK = 256
SWIGLU_LIMIT = 7.0
SWIGLU_ALPHA = 1.702

TM = 256
ADA_TN = 512
VMEM_LIMIT = 56 * 1024 * 1024


def _params(sem, vmem=None):
    return pltpu.CompilerParams(dimension_semantics=sem, vmem_limit_bytes=vmem)


def _dot(a, b):
    return jnp.dot(a, b, preferred_element_type=F32)


def _dot_nt(a, b):
    return lax.dot_general(a, b, (((1,), (1,)), ((), ())), preferred_element_type=F32)


def _dot_tn(a, b):
    return lax.dot_general(a, b, (((0,), (0,)), ((), ())), preferred_element_type=F32)


def _split2(x):
    hi = x.astype(BF16)
    lo = (x - hi.astype(F32)).astype(BF16)
    return hi, lo


def _split3(x):
    p1 = x.astype(BF16)
    r1 = x - p1.astype(F32)
    p2 = r1.astype(BF16)
    p3 = (r1 - p2.astype(F32)).astype(BF16)
    return p1, p2, p3


def _dot_hi(a, w):
    a1, a2 = _split2(a)
    w1, w2 = _split2(w)
    return _dot(a1, w1) + (_dot(a1, w2) + _dot(a2, w1))


def _dot_exact_lhs(t, x):
    p1, p2, p3 = _split3(x)
    return _dot(t, p1) + (_dot(t, p2) + _dot(t, p3))


def _sigmoid(x):
    return jax.nn.sigmoid(x)


def _silu(x):
    return x * jax.nn.sigmoid(x)


def _rms(x, g):
    return x * lax.rsqrt(jnp.mean(x * x, axis=-1, keepdims=True) + EPS) * g


def _modulate(x, g, shift, scale):
    return _rms(x, g) * (1.0 + scale) + shift


def _causal(n, reverse):
    r = lax.broadcasted_iota(I32, (n, n), 0)
    s = lax.broadcasted_iota(I32, (n, n), 1)
    return (s >= r) if reverse else (s <= r)


def _ada_body(c_ref, w_ref, b_ref, o_ref):
    a = _silu(c_ref[...])
    o_ref[0] = _dot_hi(a, w_ref[0]) + b_ref[0]


def _ada(cstack, ada_w, ada_b):
    depth, d, n6 = ada_w.shape
    nb = cstack.shape[0]
    return pl.pallas_call(
        _ada_body,
        grid=(depth, n6 // ADA_TN),
        in_specs=[
            pl.BlockSpec((nb, d), lambda l, j: (0, 0)),
            pl.BlockSpec((1, d, ADA_TN), lambda l, j: (l, 0, j)),
            pl.BlockSpec((1, 1, ADA_TN), lambda l, j: (l, 0, j)),
        ],
        out_specs=pl.BlockSpec((1, nb, ADA_TN), lambda l, j: (l, 0, j)),
        out_shape=jax.ShapeDtypeStruct((depth, nb, n6), F32),
        compiler_params=_params(("arbitrary", "arbitrary")),
        name="ada",
    )(cstack, ada_w, ada_b.reshape(depth, 1, n6))


def _hg_prep_body(x_ref, mod_ref, g_ref, w_ref, lb_ref, q_o, v_o, kf_o, kb_o, lf_o, lbw_o, g_o, *, layer, d):
    mod = mod_ref[0, 0]
    h = _modulate(x_ref[0], g_ref[...], mod[0:1], mod[1:2]).astype(BF16)

    def lower_bound(direction):
        rows = [lb_ref[direction, r : r + 1, :] for r in range(lb_ref.shape[1])]
        mx = functools.reduce(jnp.maximum, rows)
        es = [jnp.exp(r - mx) for r in rows]
        return sum(es[: layer + 1]) / sum(es)

    def proj(j):
        return _dot(h, w_ref[:, j * d : (j + 1) * d])

    q_o[0] = _silu(proj(0)).astype(BF16)
    v_o[0] = proj(1).T.astype(BF16)
    for j, k_o, l_o in ((2, kf_o, lf_o), (3, kb_o, lbw_o)):
        lb = lower_bound(j - 2)
        f = lb + (1.0 - lb) * _sigmoid(proj(j))
        k_o[0] = (1.0 - f).astype(BF16)
        l_o[0] = jnp.log(f)
    g_o[0] = proj(4).astype(BF16)


def _hg_prep(xc, mods, norm_g, w_in, lb_logits, layer, nct):
    nb, lc, d = xc.shape
    tile = pl.BlockSpec((1, TM, d), lambda b, i: (b, i, 0))
    out = lambda dt: jax.ShapeDtypeStruct((nb, lc, d), dt)
    return pl.pallas_call(
        functools.partial(_hg_prep_body, layer=layer, d=d),
        grid=(nb, lc // TM),
        in_specs=[
            tile,
            pl.BlockSpec((1, 1, 6, d), lambda b, i: (layer, jnp.where(i < nct, nb, b), 0, 0)),
            pl.BlockSpec((1, d), lambda b, i: (0, 0)),
            pl.BlockSpec((d, 5 * d), lambda b, i: (0, 0)),
            pl.BlockSpec(lb_logits.shape, lambda b, i: (0, 0, 0)),
        ],
        out_specs=[tile, pl.BlockSpec((1, d, TM), lambda b, i: (b, 0, i))] + [tile] * 5,
        out_shape=[out(BF16), jax.ShapeDtypeStruct((nb, d, lc), BF16), out(BF16), out(BF16), out(F32), out(F32),
                   out(BF16)],
        compiler_params=_params(("arbitrary", "arbitrary"), VMEM_LIMIT),
        name="hg_prep",
    )(xc, mods, norm_g.reshape(1, d), w_in, lb_logits)


HG_BLOCK = 2 * CHUNK


def _hg_scan_body(qf, vtf, kf, lf, qb, vtb, kb, lb, of, ob, sf, sb):
    @pl.when(pl.program_id(1) == 0)
    def _():
        sf[...] = jnp.zeros_like(sf)
        sb[...] = jnp.zeros_like(sb)

    n, c = HG_BLOCK, CHUNK
    r = lax.broadcasted_iota(I32, (n, n), 0)
    s = lax.broadcasted_iota(I32, (n, n), 1)
    same_chunk = (r < c) == (s < c)
    first = lax.broadcasted_iota(I32, (n, 1), 0) < c
    jobs = []
    for q_ref, vt_ref, k_ref, l_ref, o_ref, s_ref, reverse in (
            (qf, vtf, kf, lf, of, sf, False), (qb, vtb, kb, lb, ob, sb, True)):
        causal = jnp.logical_and(same_chunk, (s >= r) if reverse else (s <= r))
        bc = _dot_exact_lhs(causal.astype(BF16), l_ref[0])
        mid = (c - 1 - c // 2) if reverse else c // 2
        end = 0 if reverse else c - 1
        b_mid = jnp.where(first, bc[mid:mid + 1], bc[c + mid:c + mid + 1])
        b_end = jnp.where(first, bc[end:end + 1], bc[c + end:c + end + 1])
        q = q_ref[0].astype(F32)
        k = k_ref[0].astype(F32)
        vt = vt_ref[0]
        qd = (q * jnp.exp(bc - b_mid)).astype(BF16)
        kd = (k * jnp.exp(b_mid - bc)).astype(BF16)
        qs = q * jnp.exp(bc)
        ke = k * jnp.exp(b_end - bc)
        in_a = jnp.logical_not(first) if reverse else first
        ia, ib = (c, 0) if reverse else (0, c)
        qs_a = jnp.where(in_a, qs, 0.0).astype(BF16)
        qs_b = jnp.where(in_a, 0.0, qs).astype(BF16)
        ke_a = jnp.where(in_a, ke, 0.0).astype(BF16)
        ke_b = jnp.where(in_a, 0.0, ke).astype(BF16)
        dec_a = jnp.exp(bc[ia + end:ia + end + 1])
        dec_b = jnp.exp(bc[ib + end:ib + end + 1])
        dh = q.shape[-1] // HEADS
        for h in range(HEADS):
            sl = slice(h * dh, (h + 1) * dh)
            jobs.append(dict(h=h, sl=sl, o_ref=o_ref, s_ref=s_ref, causal=causal, qd=qd[:, sl], kd=kd[:, sl],
                             vt=vt[sl, :], qs_a=qs_a[:, sl], qs_b=qs_b[:, sl], ke_a=ke_a[:, sl], ke_b=ke_b[:, sl],
                             dec_a=dec_a[:, sl], dec_b=dec_b[:, sl]))
    for j in jobs:
        j["att"] = jnp.where(j["causal"], _dot_nt(j["qd"], j["kd"]), 0.0).astype(BF16)
        j["ua"] = _dot(j["vt"], j["ke_a"])
        j["ub"] = _dot(j["vt"], j["ke_b"])
    for j in jobs:
        s_a = j["s_ref"][j["h"]]
        s_b = s_a * j["dec_a"] + j["ua"]
        j["s_ref"][j["h"]] = s_b * j["dec_b"] + j["ub"]
        j["rhs"] = jnp.concatenate([s_a.astype(BF16), s_b.astype(BF16), j["vt"]], axis=1)
    for j in jobs:
        lhs = jnp.concatenate([j["qs_a"], j["qs_b"], j["att"]], axis=1)
        j["o_ref"][0, :, j["sl"]] = _dot_nt(lhs, j["rhs"])


def _chunk_maps(ncc, nc):
    fwd = lambda b, c: (b, c, 0)
    bwd = lambda b, c: (b, jnp.where(c < ncc, ncc - 1 - c, nc - 1 + ncc - c), 0)
    return fwd, bwd


def _hg_scan(q, vt, kf, kb, lf, lb, ctx_len):
    nb, lc, d = q.shape
    nc = lc // HG_BLOCK
    fwd, bwd = _chunk_maps(ctx_len // HG_BLOCK, nc)
    blk = lambda m: pl.BlockSpec((1, HG_BLOCK, d), m)
    blk_t = lambda m: pl.BlockSpec((1, d, HG_BLOCK), lambda b, c: (m(b, c)[0], 0, m(b, c)[1]))
    ins = lambda m: [blk(m), blk_t(m), blk(m), blk(m)]
    dh = d // HEADS
    return pl.pallas_call(
        _hg_scan_body,
        grid=(nb, nc),
        in_specs=ins(fwd) + ins(bwd),
        out_specs=[blk(fwd), blk(bwd)],
        out_shape=[jax.ShapeDtypeStruct((nb, lc, d), F32)] * 2,
        scratch_shapes=[pltpu.VMEM((HEADS, dh, dh), F32)] * 2,
        compiler_params=_params(("arbitrary", "arbitrary")),
        name="hg_scan",
    )(q, vt, kf, lf, q, vt, kb, lb)


def _post_body(of, ob, g_ref, x_ref, mod_ref, ng_ref, wo_ref, nf_ref, rw_ref, rb_ref,
               x_o, f_o, ti_o, tg_o, cnt_o, *, gate):
    @pl.when(jnp.logical_and(pl.program_id(0) == 0, pl.program_id(1) == 0))
    def _():
        cnt_o[...] = jnp.zeros_like(cnt_o)

    y = of[0] + ob[0]
    dh = y.shape[-1] // HEADS
    parts = []
    for h in range(HEADS):
        yh = y[:, h * dh : (h + 1) * dh]
        parts.append(yh * lax.rsqrt(jnp.mean(yh * yh, axis=-1, keepdims=True) + EPS))
    gt = g_ref[0].astype(F32)
    act = _silu(gt) if gate == "silu" else _sigmoid(gt)
    yn = jnp.concatenate(parts, axis=-1) * ng_ref[...] * act
    mod = mod_ref[0, 0]
    xn = x_ref[0] + mod[2:3] * _dot(yn.astype(BF16), wo_ref[...])
    x_o[0] = xn
    f = _modulate(xn, nf_ref[...], mod[3:4], mod[4:5])
    f_o[0] = f
    vals = _dot_hi(f, rw_ref[...]) + rb_ref[...]
    lane = lax.broadcasted_iota(I32, vals.shape, 1)
    tops, ids = [], []
    chosen = jnp.zeros(vals.shape, F32)
    for _ in range(TOP_K):
        m = jnp.max(vals, axis=-1, keepdims=True)
        idx = jnp.min(jnp.where(vals == m, lane, N_EXPERTS), axis=-1, keepdims=True)
        tops.append(m)
        ids.append(idx)
        hit = lane == idx
        chosen += hit.astype(F32)
        vals = jnp.where(hit, -jnp.inf, vals)
    cnt_o[...] += jnp.sum(chosen, axis=0, keepdims=True)
    es = [jnp.exp(m - tops[0]) for m in tops]
    tot = functools.reduce(lambda a, b: a + b, es)
    tg_o[0] = jnp.concatenate([e / tot for e in es], axis=-1)
    ti_o[0] = jnp.concatenate(ids, axis=-1)


def _post(of, ob, g, xc, mods, norm_g, w_out, norm_ffn_g, router_w, router_b, layer, nct, skip, gate):
    nb, lc, d = xc.shape
    nt = lc // TM - skip
    tile = pl.BlockSpec((1, TM, d), lambda b, i: (b, i + skip, 0))
    otile = pl.BlockSpec((1, TM, d), lambda b, i: (b, i, 0))
    small = pl.BlockSpec((1, TM, TOP_K), lambda b, i: (b, i, 0))
    row = pl.BlockSpec((1, d), lambda b, i: (0, 0))
    return pl.pallas_call(
        functools.partial(_post_body, gate=gate),
        grid=(nb, nt),
        in_specs=[
            tile, tile, tile, tile,
            pl.BlockSpec((1, 1, 6, d), lambda b, i: (layer, jnp.where(i + skip < nct, nb, b), 0, 0)),
            row,
            pl.BlockSpec((d, d), lambda b, i: (0, 0)),
            row,
            pl.BlockSpec((d, N_EXPERTS), lambda b, i: (0, 0)),
            pl.BlockSpec((1, N_EXPERTS), lambda b, i: (0, 0)),
        ],
        out_specs=[tile, otile, small, small, pl.BlockSpec((1, N_EXPERTS), lambda b, i: (0, 0))],
        out_shape=[
            jax.ShapeDtypeStruct((nb, lc, d), F32),
            jax.ShapeDtypeStruct((nb, nt * TM, d), F32),
            jax.ShapeDtypeStruct((nb, nt * TM, TOP_K), I32),
            jax.ShapeDtypeStruct((nb, nt * TM, TOP_K), F32),
            jax.ShapeDtypeStruct((1, N_EXPERTS), F32),
        ],
        input_output_aliases={3: 0},
        compiler_params=_params(("arbitrary", "arbitrary")),
        name="post_mixer",
    )(of, ob, g, xc, mods, norm_g.reshape(1, d), w_out, norm_ffn_g.reshape(1, d), router_w,
      router_b.reshape(1, N_EXPERTS))


def _rank_body(ti_ref, cnt_ref, dest_o, pend_o, carry):
    @pl.when(pl.program_id(0) == 0)
    def _():
        padded = jnp.ceil(cnt_ref[...] * (1.0 / MOE_BLOCK)) * MOE_BLOCK
        r = lax.broadcasted_iota(I32, (N_EXPERTS, N_EXPERTS), 0)
        s = lax.broadcasted_iota(I32, (N_EXPERTS, N_EXPERTS), 1)
        upper = (r <= s).astype(BF16)
        p1, p2, p3 = _split3(jnp.broadcast_to(padded, (8, N_EXPERTS)))
        pends = (_dot(p1, upper) + (_dot(p2, upper) + _dot(p3, upper)))[0:1]
        pend_o[...] = pends
        carry[...] = pends - padded

    ti = ti_ref[...]
    lane = lax.broadcasted_iota(I32, (TM, N_EXPERTS), 1)
    hots = [lane == ti[:, k : k + 1] for k in range(TOP_K)]
    chosen = functools.reduce(lambda a, b: a + b, [h.astype(F32) for h in hots])
    r = lax.broadcasted_iota(I32, (TM, TM), 0)
    s = lax.broadcasted_iota(I32, (TM, TM), 1)
    row = _dot((s < r).astype(BF16), chosen.astype(BF16)) + carry[...]
    dest = [jnp.sum(jnp.where(h, row, 0.0), axis=-1, keepdims=True) for h in hots]
    dest_o[...] = jnp.concatenate(dest, axis=-1).astype(I32)
    carry[...] += jnp.sum(chosen, axis=0, keepdims=True)


def _rank(ti, cnt):
    t = ti.shape[0]
    small = pl.BlockSpec((1, N_EXPERTS), lambda i: (0, 0))
    return pl.pallas_call(
        _rank_body,
        grid=(t // TM,),
        in_specs=[pl.BlockSpec((TM, TOP_K), lambda i: (i, 0)), small],
        out_specs=[pl.BlockSpec((TM, TOP_K), lambda i: (i, 0)), small],
        out_shape=[jax.ShapeDtypeStruct((t, TOP_K), I32), jax.ShapeDtypeStruct((1, N_EXPERTS), F32)],
        scratch_shapes=[pltpu.VMEM((1, N_EXPERTS), F32)],
        compiler_params=_params(("arbitrary",)),
        name="moe_rank",
    )(ti, cnt)


def _row_copy(src, i, dst, j, sem):
    return pltpu.make_async_copy(src.at[pl.ds(i, 1)], dst.at[pl.ds(j, 1)], sem)


def _dispatch_body(ps_ref, cnt_ref, pe_ref, dest_ref, f_ref, xs_hbm, zrow, sem, zsem, *, n_rows, n_pad):
    i = pl.program_id(0)

    for t in range(TM):
        for k in range(TOP_K):
            _row_copy(f_ref, t, xs_hbm, dest_ref[t * TOP_K + k], sem).start()

    @pl.when(i == 0)
    def _():
        zrow[...] = jnp.zeros_like(zrow)

        def zero(r, carry):
            _row_copy(zrow, 0, xs_hbm, r, zsem).start()
            return carry

        def per_expert(e, carry):
            lax.fori_loop(ps_ref[e] + cnt_ref[e], pe_ref[e], zero, 0)
            return carry

        lax.fori_loop(0, N_EXPERTS, per_expert, 0)
        lax.fori_loop(pe_ref[N_EXPERTS - 1], n_rows, zero, 0)

        def wait_zero(r, carry):
            _row_copy(zrow, 0, xs_hbm, 0, zsem).wait()
            return carry

        lax.fori_loop(0, n_pad, wait_zero, 0)

    pltpu.make_async_copy(xs_hbm.at[pl.ds(0, TM * TOP_K)], xs_hbm.at[pl.ds(0, TM * TOP_K)], sem).wait()


def _dispatch(pstarts, counts, pends, dest_flat, f, n_rows):
    t, d = f.shape
    n_slots = TM * TOP_K
    return pl.pallas_call(
        functools.partial(_dispatch_body, n_rows=n_rows, n_pad=n_rows - t * TOP_K),
        grid_spec=pltpu.PrefetchScalarGridSpec(
            num_scalar_prefetch=3,
            grid=(t // TM,),
            in_specs=[
                pl.BlockSpec((n_slots,), lambda i, *_: (i,), memory_space=pltpu.SMEM),
                pl.BlockSpec((TM, d), lambda i, *_: (i, 0)),
            ],
            out_specs=pl.BlockSpec(memory_space=pl.ANY),
            scratch_shapes=[pltpu.VMEM((8, d), F32), pltpu.SemaphoreType.DMA, pltpu.SemaphoreType.DMA],
        ),
        out_shape=jax.ShapeDtypeStruct((n_rows, d), F32),
        compiler_params=_params(("arbitrary",)),
        name="moe_dispatch",
    )(pstarts, counts, pends, dest_flat, f)


def _ffn_body(be_ref, nu_ref, xs_ref, w1_ref, b1_ref, w2_ref, b2_ref, ys_ref, w1b, w2b):
    b = pl.program_id(0)
    changed = jnp.logical_or(b == 0, be_ref[b] != be_ref[jnp.maximum(b - 1, 0)])

    @pl.when(changed)
    def _():
        w1b[...] = w1_ref[0].astype(BF16)
        w2b[...] = w2_ref[0].astype(BF16)

    @pl.when(b < nu_ref[0])
    def _():
        u = _dot(xs_ref[...].astype(BF16), w1b[...]) + b1_ref[0]
        dff = u.shape[-1] // 2
        glu = jnp.minimum(u[:, :dff], SWIGLU_LIMIT)
        lin = jnp.clip(u[:, dff:], -SWIGLU_LIMIT, SWIGLU_LIMIT)
        y = glu * _sigmoid(SWIGLU_ALPHA * glu) * (lin + 1.0)
        ys_ref[...] = _dot(y.astype(BF16), w2b[...]) + b2_ref[0]

    @pl.when(b >= nu_ref[0])
    def _():
        ys_ref[...] = jnp.zeros_like(ys_ref)


def _ffn(block_e, n_used, xs, w1, b1, w2, b2, layer):
    n_rows, d = xs.shape
    depth, ne, _, dff2 = w1.shape
    b1 = b1.reshape(depth, ne, 1, dff2)
    b2 = b2.reshape(depth, ne, 1, d)
    wmap = lambda b, be, nu: (layer, be[b], 0, 0)
    return pl.pallas_call(
        _ffn_body,
        grid_spec=pltpu.PrefetchScalarGridSpec(
            num_scalar_prefetch=2,
            grid=(n_rows // MOE_BLOCK,),
            in_specs=[
                pl.BlockSpec((MOE_BLOCK, d), lambda b, be, nu: (b, 0)),
                pl.BlockSpec((None, 1, d, dff2), wmap),
                pl.BlockSpec((None, 1, 1, dff2), wmap),
                pl.BlockSpec((None, 1, dff2 // 2, d), wmap),
                pl.BlockSpec((None, 1, 1, d), wmap),
            ],
            out_specs=pl.BlockSpec((MOE_BLOCK, d), lambda b, be, nu: (b, 0)),
            scratch_shapes=[pltpu.VMEM((d, dff2), BF16), pltpu.VMEM((dff2 // 2, d), BF16)],
        ),
        out_shape=jax.ShapeDtypeStruct((n_rows, d), F32),
        compiler_params=_params(("arbitrary",), VMEM_LIMIT),
        name="moe_ffn",
    )(block_e, n_used, xs, w1, b1, w2, b2)


def _combine_body(dest_ref, next_ref, tg_ref, ys_hbm, x_ref, mod_ref, fg_ref, o_ref, buf, sem, *, final):
    n = pl.program_id(0) * pl.num_programs(1) + pl.program_id(1)
    total = pl.num_programs(0) * pl.num_programs(1)

    def gather(idx_ref, slot):
        for t in range(TM):
            for k in range(TOP_K):
                _row_copy(ys_hbm, idx_ref[t * TOP_K + k], buf.at[slot], k * TM + t, sem.at[slot]).start()

    def step(slot):
        if slot == 0:
            @pl.when(n == 0)
            def _():
                gather(dest_ref, 0)

        @pl.when(n + 1 < total)
        def _():
            gather(next_ref, 1 - slot)

        pltpu.make_async_copy(ys_hbm.at[pl.ds(0, TM * TOP_K)], buf.at[slot], sem.at[slot]).wait()
        g = tg_ref[0]
        acc = g[:, 0:1] * buf[slot, 0:TM]
        for k in range(1, TOP_K):
            acc += g[:, k : k + 1] * buf[slot, k * TM : (k + 1) * TM]
        xn = x_ref[0] + mod_ref[0, 0][5:6] * acc
        o_ref[0] = _rms(xn, fg_ref[...]) if final else xn

    for slot in range(2):
        pl.when(n % 2 == slot)(functools.partial(step, slot))


def _combine(dest_flat, tg, ys, xc, mods, final_g, layer, nct, skip, final):
    nb, lc, d = xc.shape
    nt = lc // TM - skip
    n_slots = TM * TOP_K
    tile = pl.BlockSpec((1, TM, d), lambda b, i: (b, i + skip, 0))
    if final:
        out_spec = pl.BlockSpec((1, TM, d), lambda b, i: (b, i, 0))
        out_shape = jax.ShapeDtypeStruct((nb, nt * TM, d), F32)
        aliases = {}
    else:
        out_spec, out_shape, aliases = tile, jax.ShapeDtypeStruct((nb, lc, d), F32), {4: 0}
    return pl.pallas_call(
        functools.partial(_combine_body, final=final),
        grid=(nb, nt),
        in_specs=[
            pl.BlockSpec((n_slots,), lambda b, i: (b * nt + i,), memory_space=pltpu.SMEM),
            pl.BlockSpec((n_slots,), lambda b, i: (jnp.minimum(b * nt + i + 1, nb * nt - 1),),
                         memory_space=pltpu.SMEM),
            pl.BlockSpec((1, TM, TOP_K), lambda b, i: (b, i, 0)),
            pl.BlockSpec(memory_space=pl.ANY),
            tile,
            pl.BlockSpec((1, 1, 6, d), lambda b, i: (layer, jnp.where(i + skip < nct, nb, b), 0, 0)),
            pl.BlockSpec((1, d), lambda b, i: (0, 0)),
        ],
        out_specs=out_spec,
        scratch_shapes=[pltpu.VMEM((2, n_slots, d), F32), pltpu.SemaphoreType.DMA((2,))],
        out_shape=out_shape,
        input_output_aliases=aliases,
        compiler_params=_params(("arbitrary", "arbitrary"), VMEM_LIMIT),
        name="moe_combine",
    )(dest_flat, dest_flat, tg, ys, xc, mods, final_g.reshape(1, d))


def _moe(f, ti, tg, cnt, xc, mods, w1, b1, w2, b2, final_g, layer, nct, skip, final):
    nb, ntok, d = f.shape
    t = nb * ntok
    dest, pend = _rank(ti.reshape(t, TOP_K), cnt)
    counts = cnt[0].astype(I32)
    pends = pend[0].astype(I32)
    pstarts = pends - (counts + MOE_BLOCK - 1) // MOE_BLOCK * MOE_BLOCK
    n_slots = t * TOP_K
    n_rows = -(-(n_slots + N_EXPERTS * (MOE_BLOCK - 1)) // MOE_BLOCK) * MOE_BLOCK
    n_blocks = n_rows // MOE_BLOCK
    block_start = jnp.arange(n_blocks, dtype=I32) * MOE_BLOCK
    block_e = jnp.minimum(jnp.sum((pends[None, :] <= block_start[:, None]).astype(I32), axis=1), N_EXPERTS - 1)
    n_used = (pends[-1:] // MOE_BLOCK).astype(I32)
    dest_flat = dest.reshape(n_slots)
    xs = _dispatch(pstarts, counts, pends, dest_flat, f.reshape(t, d), n_rows)
    ys = _ffn(block_e, n_used, xs, w1, b1, w2, b2, layer)
    return _combine(dest_flat, tg, ys, xc, mods, final_g, layer, nct, skip, final)


def _ml_prep_body(x_ref, mod_ref, g_ref, wqk_ref, wv_ref, wo_ref, wg_ref, bg_ref,
                  qk_o, v_o, o_o, gi_o, gf_o, git_o, gft_o):
    mod = mod_ref[0, 0]
    h = _modulate(x_ref[0], g_ref[...], mod[0:1], mod[1:2])
    hb = h.astype(BF16)
    qk_o[0] = _dot(hb, wqk_ref[...])
    v_o[0] = _dot(hb, wv_ref[...]).astype(BF16)
    o_o[0] = _dot(hb, wo_ref[...]).astype(BF16)
    gates = GATE_CAP * jnp.tanh((_dot_hi(h, wg_ref[...]) + bg_ref[...]) / GATE_CAP)
    ng = gates.shape[-1] // 2
    fg = gates[:, ng:]
    logf = jnp.minimum(fg, 0.0) - jnp.log1p(jnp.exp(-jnp.abs(fg)))
    pad = jnp.zeros((gates.shape[0], gi_o.shape[-1] - ng), F32)
    gi = jnp.concatenate([gates[:, :ng], pad], axis=-1)
    gf = jnp.concatenate([logf, pad], axis=-1)
    gi_o[0] = gi
    gf_o[0] = gf
    git_o[0] = gi.T[:ng]
    gft_o[0] = gf.T[:ng]


def _ml_prep(xc, mods, norm_g, wqk, wv, wo, wg, bg, layer, nct):
    nb, lc, d = xc.shape
    ng = wg.shape[1] // 2
    tile = lambda w: pl.BlockSpec((1, TM, w), lambda b, i: (b, i, 0))
    full = lambda a: pl.BlockSpec(a.shape, lambda b, i: (0,) * a.ndim)
    return pl.pallas_call(
        _ml_prep_body,
        grid=(nb, lc // TM),
        in_specs=[
            tile(d),
            pl.BlockSpec((1, 1, 6, d), lambda b, i: (layer, jnp.where(i < nct, nb, b), 0, 0)),
            pl.BlockSpec((1, d), lambda b, i: (0, 0)),
            full(wqk), full(wv), full(wo), full(wg), full(bg),
        ],
        out_specs=[tile(wqk.shape[1]), tile(wv.shape[1]), tile(wo.shape[1]), tile(128), tile(128),
                   pl.BlockSpec((1, ng, TM), lambda b, i: (b, 0, i)), pl.BlockSpec((1, ng, TM), lambda b, i: (b, 0, i))],
        out_shape=[
            jax.ShapeDtypeStruct((nb, lc, wqk.shape[1]), F32),
            jax.ShapeDtypeStruct((nb, lc, wv.shape[1]), BF16),
            jax.ShapeDtypeStruct((nb, lc, wo.shape[1]), BF16),
            jax.ShapeDtypeStruct((nb, lc, 128), F32),
            jax.ShapeDtypeStruct((nb, lc, 128), F32),
            jax.ShapeDtypeStruct((nb, ng, lc), F32),
            jax.ShapeDtypeStruct((nb, ng, lc), F32),
        ],
        compiler_params=_params(("arbitrary", "arbitrary"), VMEM_LIMIT),
        name="ml_prep",
    )(xc, mods, norm_g.reshape(1, d), wqk, wv, wo, wg, bg)


CONV_PAD = 72


def _conv_body(main, prev, nxt, cw_ref, cb_ref, q_o, k_o, sc, *, nct, dk):
    i = pl.program_id(1)
    last = pl.num_programs(1) - 1
    p = CONV_PAD
    w = main.shape[-1]
    sc[0:p - GRID_W] = jnp.zeros((p - GRID_W, w), F32)
    sc[p + TM + GRID_W:] = jnp.zeros((sc.shape[0] - p - TM - GRID_W, w), F32)
    sc[p - GRID_W:p] = jnp.where(i > nct, prev[0], 0.0)
    sc[p + TM:p + TM + GRID_W] = jnp.where(jnp.logical_and(i >= nct, i < last), nxt[0], 0.0)
    sc[p:p + TM] = main[0]
    col = lax.broadcasted_iota(I32, (TM, 1), 0) & (GRID_W - 1)

    def finish(acc):
        y = _silu(acc)
        kt = y[:, w // 2:].T
        qs, ks = [], []
        for h in range(HEADS):
            qs += [y[:, h * dk:(h + 1) * dk] * (dk ** -0.5), jnp.zeros((TM, 128 - dk), F32)]
            ks += [kt[h * dk:(h + 1) * dk], jnp.zeros((128 - dk, TM), F32)]
        q_o[0] = jnp.concatenate(qs, axis=1).astype(BF16)
        k_o[0] = jnp.concatenate(ks, axis=0).astype(BF16)

    @pl.when(i < nct)
    def _():
        acc = cb_ref[...] + sc[p:p + TM] * cw_ref[4:5]
        acc += sc[p - 1:p - 1 + TM] * cw_ref[3:4] + sc[p + 1:p + 1 + TM] * cw_ref[5:6]
        finish(acc)

    @pl.when(i >= nct)
    def _():
        acc = jnp.broadcast_to(cb_ref[...], (TM, w))
        for dy in range(3):
            for dx in range(3):
                off = p + (dy - 1) * GRID_W + (dx - 1)
                u = sc[off:off + TM]
                if dx == 0:
                    u = jnp.where(col != 0, u, 0.0)
                if dx == 2:
                    u = jnp.where(col != GRID_W - 1, u, 0.0)
                acc += u * cw_ref[dy * 3 + dx:dy * 3 + dx + 1]
        finish(acc)


def _conv(qk, conv_w, conv_b, nct):
    nb, lc, w = qk.shape
    nt = lc // TM
    per = TM // GRID_W
    nh = lc // GRID_W
    return pl.pallas_call(
        functools.partial(_conv_body, nct=nct, dk=w // 2 // HEADS),
        grid=(nb, nt),
        in_specs=[
            pl.BlockSpec((1, TM, w), lambda b, i: (b, i, 0)),
            pl.BlockSpec((1, GRID_W, w), lambda b, i: (b, jnp.maximum(i * per - 1, 0), 0)),
            pl.BlockSpec((1, GRID_W, w), lambda b, i: (b, jnp.minimum(i * per + per, nh - 1), 0)),
            pl.BlockSpec((9, w), lambda b, i: (0, 0)),
            pl.BlockSpec((1, w), lambda b, i: (0, 0)),
        ],
        out_specs=[pl.BlockSpec((1, TM, HEADS * 128), lambda b, i: (b, i, 0)),
                   pl.BlockSpec((1, HEADS * 128, TM), lambda b, i: (b, 0, i))],
        out_shape=[jax.ShapeDtypeStruct((nb, lc, HEADS * 128), BF16),
                   jax.ShapeDtypeStruct((nb, HEADS * 128, lc), BF16)],
        scratch_shapes=[pltpu.VMEM((CONV_PAD + TM + GRID_W + 8, w), F32)],
        compiler_params=_params(("arbitrary", "arbitrary")),
        name="ml_conv",
    )(qk, qk, qk, conv_w.reshape(9, w), conv_b.reshape(1, w))


ML_CHUNK = 128


def _dot_exact_rhs(x, t):
    p1, p2, p3 = _split3(x)
    return _dot(p1, t) + (_dot(p2, t) + _dot(p3, t))


def _ml_scan_body(qf, ktf, vf, gif, gff, gitf, gftf, qb, ktb, vb, gib, gfb, gitb, gftb,
                  of, ob, c_ref, mrow_ref, mcol_ref):
    c = ML_CHUNK
    nd = 2 * HEADS

    @pl.when(pl.program_id(1) == 0)
    def _():
        for ref in (c_ref, mrow_ref, mcol_ref):
            ref[...] = jnp.zeros_like(ref)

    low = _causal(c, False)
    upp = _causal(c, True)
    low_b, upp_b = low.astype(BF16), upp.astype(BF16)
    neg = -jnp.inf

    lane = lax.broadcasted_iota(I32, (c, 128), 1)
    tpos = lax.broadcasted_iota(I32, (c, 128), 0)
    fwd_l = lane < HEADS
    gi = jnp.where(fwd_l, gif[0], gib[0])
    gf = jnp.where(fwd_l, gff[0], gfb[0])
    bs = jnp.where(fwd_l, _dot_exact_lhs(low_b, gf), _dot_exact_lhs(upp_b, gf))
    a = gi - bs
    pre, suf = a, a
    step = 1
    while step < c:
        pre = jnp.maximum(pre, jnp.where(tpos >= step, pltpu.roll(pre, step, 0), neg))
        suf = jnp.maximum(suf, jnp.where(tpos < c - step, pltpu.roll(suf, c - step, 0), neg))
        step *= 2
    m0 = mrow_ref[0:1]
    mm = jnp.maximum(m0, jnp.where(fwd_l, pre, suf))
    w_inter = jnp.exp(m0 - mm)
    floor = jnp.exp(-(bs + mm))
    fwd_r = fwd_l[0:1]
    b_end = jnp.where(fwd_r, bs[c - 1:c], bs[0:1])
    m_end = jnp.where(fwd_r, mm[c - 1:c], mm[0:1])
    decay = jnp.exp(m0 - m_end)
    mrow_ref[...] = jnp.broadcast_to(b_end + m_end, mrow_ref.shape)

    rowj = lax.broadcasted_iota(I32, (nd, c), 0)
    fwd_t = rowj < HEADS
    git = jnp.where(fwd_t, gitf[0], gitb[0])
    gft = jnp.where(fwd_t, gftf[0], gftb[0])
    bst = jnp.where(fwd_t, _dot_exact_rhs(gft, upp_b), _dot_exact_rhs(gft, low_b))
    at = git - bst
    m0c = mcol_ref[...]
    m_end_c = jnp.maximum(m0c[:, 0:1], jnp.max(at, axis=-1, keepdims=True))
    wk_t = jnp.exp(at - m_end_c)
    b_end_c = jnp.where(fwd_t[:, 0:1], bst[:, c - 1:c], bst[:, 0:1])
    mcol_ref[...] = jnp.broadcast_to(b_end_c + m_end_c, mcol_ref.shape)

    wide = HEADS * 128
    head_of_lane = lax.shift_right_logical(lax.broadcasted_iota(I32, (nd, wide), 1), 7)
    row_w = lax.broadcasted_iota(I32, (nd, wide), 0)
    spos = lax.broadcasted_iota(I32, (c, wide), 1) & 127
    tpos_w = lax.broadcasted_iota(I32, (c, wide), 0)
    pad_rows = jnp.zeros((128 - nd, wide), BF16)
    at_parts = _split3(at)
    mm_parts = _split3(mm)
    ones_blk = jnp.ones((c, 128), BF16)
    dirs = ((0, qf, ktf, vf, of, spos <= tpos_w), (1, qb, ktb, vb, ob, spos >= tpos_w))
    jobs = []
    for d, q_ref, kt_ref, v_ref, o_ref, causal_w in dirs:
        sel = row_w == HEADS * d + head_of_lane
        sel_b = jnp.concatenate([sel.astype(BF16), pad_rows], axis=0)
        tile8 = lambda x: jnp.concatenate([x] * HEADS, axis=1)
        a_rows = [jnp.where(sel, tile8(p), 0).astype(BF16) for p in at_parts]
        zero_rows = jnp.zeros((128 - 3 * nd, wide), BF16)
        rhs = jnp.concatenate(a_rows + [zero_rows, sel_b, sel_b, sel_b], axis=0)
        lhs = jnp.concatenate(
            [(lane < 3 * nd).astype(BF16)] + [(-p).astype(BF16) for p in mm_parts], axis=1)
        w_intra = jnp.where(causal_w, jnp.exp(_dot(lhs, rhs)), 0.0)
        w_inter_w = _dot(w_inter.astype(BF16), sel_b)
        fl_hi, fl_lo = _split2(floor)
        floor_w = _dot(fl_hi, sel_b) + _dot(fl_lo, sel_b)
        dc_hi, dc_lo = _split2(jnp.broadcast_to(decay, (8, 128)))
        decay_w = (_dot(dc_hi, sel_b) + _dot(dc_lo, sel_b))[0:1]
        q = q_ref[0]
        kt = kt_ref[0]
        v = v_ref[0]
        wq = (w_inter_w * q.astype(F32)).astype(BF16)
        for h in range(HEADS):
            sl = slice(h * 128, (h + 1) * 128)
            jobs.append(dict(d=d, h=h, sl=sl, o_ref=o_ref, q=q[:, sl], kt=kt[sl, :], wq=wq[:, sl],
                             w_intra=w_intra[:, sl], floor=floor_w[:, sl], wk=wk_t[HEADS * d + h:HEADS * d + h + 1, :],
                             v_aug=jnp.concatenate([v[:, sl], ones_blk], axis=1),
                             dec=jnp.concatenate([decay_w[:, sl], decay_w[:, sl]], axis=1)))
    for j in jobs:
        j["sc"] = (_dot(j["q"], j["kt"]) * j["w_intra"]).astype(BF16)
    for j in jobs:
        st = c_ref[j["d"], j["h"]]
        out = _dot(jnp.concatenate([j["sc"], j["wq"]], axis=1),
                   jnp.concatenate([j["v_aug"], st.astype(BF16)], axis=0))
        j["o_ref"][0, :, j["sl"]] = out[:, :128] / jnp.maximum(jnp.abs(out[:, 128:]), j["floor"])
    for j in jobs:
        kw_t = (j["kt"].astype(F32) * j["wk"]).astype(BF16)
        c_ref[j["d"], j["h"]] = j["dec"] * c_ref[j["d"], j["h"]] + _dot(kw_t, j["v_aug"])


def _ml_scan(q, kt, v, gi, gf, git, gft, ctx_len):
    nb, lc, wide = q.shape
    dvv = v.shape[-1]
    nc = lc // ML_CHUNK
    fwd, bwd = _chunk_maps(ctx_len // ML_CHUNK, nc)
    swap = lambda m: (lambda b, c: (m(b, c)[0], 0, m(b, c)[1]))
    tm = lambda w, m: pl.BlockSpec((1, ML_CHUNK, w), m)
    ft = lambda r, m: pl.BlockSpec((1, r, ML_CHUNK), swap(m))
    ng = git.shape[1]
    ins = lambda m: [tm(wide, m), ft(wide, m), tm(dvv, m), tm(128, m), tm(128, m), ft(ng, m), ft(ng, m)]
    return pl.pallas_call(
        _ml_scan_body,
        grid=(nb, nc),
        in_specs=ins(fwd) + ins(bwd),
        out_specs=[tm(dvv, fwd), tm(dvv, bwd)],
        out_shape=[jax.ShapeDtypeStruct((nb, lc, dvv), F32)] * 2,
        scratch_shapes=[pltpu.VMEM((2, HEADS, 128, 256), F32), pltpu.VMEM((8, 128), F32),
                        pltpu.VMEM((ng, 128), F32)],
        compiler_params=_params(("arbitrary", "arbitrary"), VMEM_LIMIT),
        name="ml_scan",
    )(q, kt, v, gi, gf, git, gft, q, kt, v, gi, gf, git, gft)


def kernel(x, c, ctx, c_ctx, ada_w, ada_b, norm_mix_g, norm_ffn_g, hg_w_in, hg_lb_logits, hg_norm_g, hg_w_out,
           ml_w_in, ml_b_gate, ml_conv_w, ml_conv_b, ml_norm_g, ml_w_out, router_w, router_b,
           moe_w1, moe_b1, moe_w2, moe_b2, final_norm_g):
    nb, seq, d = x.shape
    ctx_len = ctx.shape[1]
    depth = ada_w.shape[0]
    assert ctx_len == TM and seq % TM == 0 and seq % GRID_W == 0 and d % (HEADS * 128) == 0
    nct = ctx_len // TM

    xc = jnp.concatenate([ctx, x], axis=1)
    rows = -(-(nb + 1) // 8) * 8
    cstack = jnp.concatenate([c, c_ctx[None, :], jnp.zeros((rows - nb - 1, d), F32)], axis=0)
    mods = _ada(cstack, ada_w, ada_b).reshape(depth, rows, 6, d)

    for layer in range(depth):
        j = layer // 2
        last = layer == depth - 1
        if layer % 2 == 0:
            q, vt, kf, kb, lf, lb, g = _hg_prep(xc, mods, norm_mix_g[layer], hg_w_in[j].astype(BF16),
                                                hg_lb_logits, layer, nct)
            of, ob = _hg_scan(q, vt, kf, kb, lf, lb, ctx_len)
            norm_g, w_out, gate = hg_norm_g[j], hg_w_out[j], "silu"
        else:
            w = ml_w_in[j]
            dqk = ml_conv_w.shape[-1]
            dv = ml_w_out.shape[1]
            qk, v, g, gi, gf, git, gft = _ml_prep(
                xc, mods, norm_mix_g[layer], w[:, :dqk].astype(BF16), w[:, dqk:dqk + dv].astype(BF16),
                w[:, dqk + dv:dqk + 2 * dv].astype(BF16), w[:, dqk + 2 * dv:], ml_b_gate[j][None, :], layer, nct)
            q, kt = _conv(qk, ml_conv_w[j], ml_conv_b[j], nct)
            of, ob = _ml_scan(q, kt, v, gi, gf, git, gft, ctx_len)
            norm_g, w_out, gate = ml_norm_g[j], ml_w_out[j], "sigmoid"
        skip = nct if last else 0
        xc, f, ti, tg, cnt = _post(of, ob, g, xc, mods, norm_g, w_out.astype(BF16), norm_ffn_g[layer],
                                   router_w[layer], router_b[layer], layer, nct, skip, gate)
        xc = _moe(f, ti, tg, cnt, xc, mods, moe_w1, moe_b1, moe_w2, moe_b2, final_norm_g, layer, nct, skip, last)
    return xc
```

```python
import functools

import jax
import jax.numpy as jnp
from jax import lax
from jax.experimental import pallas as pl
from jax.experimental.pallas import tpu as pltpu

F32 = jnp.float32
BF16 = jnp.bfloat16
I32 = jnp.int32

EPS = 1e-6
HEADS = 8
CHUNK = 64
GATE_CAP = 15.0
GRID_W = 64
N_EXPERTS = 32
TOP_K = 4
MOE_BLOCK = 256
SWIGLU_LIMIT = 7.0
SWIGLU_ALPHA = 1.702

TM = 256
ADA_TN = 512
VMEM_LIMIT = 56 * 1024 * 1024


def _params(sem, vmem=None):
    return pltpu.CompilerParams(dimension_semantics=sem, vmem_limit_bytes=vmem)


def _dot(a, b):
    return jnp.dot(a, b, preferred_element_type=F32)


def _dot_nt(a, b):
    return lax.dot_general(a, b, (((1,), (1,)), ((), ())), preferred_element_type=F32)


def _dot_tn(a, b):
    return lax.dot_general(a, b, (((0,), (0,)), ((), ())), preferred_element_type=F32)


def _split2(x):
    hi = x.astype(BF16)
    lo = (x - hi.astype(F32)).astype(BF16)
    return hi, lo


def _split3(x):
    p1 = x.astype(BF16)
    r1 = x - p1.astype(F32)
    p2 = r1.astype(BF16)
    p3 = (r1 - p2.astype(F32)).astype(BF16)
    return p1, p2, p3


def _dot_hi(a, w):
    a1, a2 = _split2(a)
    w1, w2 = _split2(w)
    return _dot(a1, w1) + (_dot(a1, w2) + _dot(a2, w1))


def _dot_exact_lhs(t, x):
    p1, p2, p3 = _split3(x)
    return _dot(t, p1) + (_dot(t, p2) + _dot(t, p3))


def _sigmoid(x):
    return jax.nn.sigmoid(x)


def _silu(x):
    return x * jax.nn.sigmoid(x)


def _rms(x, g):
    return x * lax.rsqrt(jnp.mean(x * x, axis=-1, keepdims=True) + EPS) * g


def _modulate(x, g, shift, scale):
    return _rms(x, g) * (1.0 + scale) + shift


def _causal(n, reverse):
    r = lax.broadcasted_iota(I32, (n, n), 0)
    s = lax.broadcasted_iota(I32, (n, n), 1)
    return (s >= r) if reverse else (s <= r)


def _ada_body(c_ref, w_ref, b_ref, o_ref):
    a = _silu(c_ref[...])
    o_ref[0] = _dot_hi(a, w_ref[0]) + b_ref[0]


def _ada(cstack, ada_w, ada_b):
    depth, d, n6 = ada_w.shape
    nb = cstack.shape[0]
    return pl.pallas_call(
        _ada_body,
        grid=(depth, n6 // ADA_TN),
        in_specs=[
            pl.BlockSpec((nb, d), lambda l, j: (0, 0)),
            pl.BlockSpec((1, d, ADA_TN), lambda l, j: (l, 0, j)),
            pl.BlockSpec((1, 1, ADA_TN), lambda l, j: (l, 0, j)),
        ],
        out_specs=pl.BlockSpec((1, nb, ADA_TN), lambda l, j: (l, 0, j)),
        out_shape=jax.ShapeDtypeStruct((depth, nb, n6), F32),
        compiler_params=_params(("arbitrary", "arbitrary")),
        name="ada",
    )(cstack, ada_w, ada_b.reshape(depth, 1, n6))


def _hg_prep_body(x_ref, mod_ref, g_ref, w_ref, lb_ref, q_o, v_o, kf_o, kb_o, lf_o, lbw_o, g_o, *, layer, d):
    mod = mod_ref[0, 0]
    h = _modulate(x_ref[0], g_ref[...], mod[0:1], mod[1:2]).astype(BF16)

    def lower_bound(direction):
        rows = [lb_ref[direction, r : r + 1, :] for r in range(lb_ref.shape[1])]
        mx = functools.reduce(jnp.maximum, rows)
        es = [jnp.exp(r - mx) for r in rows]
        return sum(es[: layer + 1]) / sum(es)

    def proj(j):
        return _dot(h, w_ref[:, j * d : (j + 1) * d])

    q_o[0] = _silu(proj(0)).astype(BF16)
    v_o[0] = proj(1).T.astype(BF16)
    for j, k_o, l_o in ((2, kf_o, lf_o), (3, kb_o, lbw_o)):
        lb = lower_bound(j - 2)
        f = lb + (1.0 - lb) * _sigmoid(proj(j))
        k_o[0] = (1.0 - f).astype(BF16)
        l_o[0] = jnp.log(f)
    g_o[0] = proj(4).astype(BF16)


def _hg_prep(xc, mods, norm_g, w_in, lb_logits, layer, nct):
    nb, lc, d = xc.shape
    tile = pl.BlockSpec((1, TM, d), lambda b, i: (b, i, 0))
    out = lambda dt: jax.ShapeDtypeStruct((nb, lc, d), dt)
    return pl.pallas_call(
        functools.partial(_hg_prep_body, layer=layer, d=d),
        grid=(nb, lc // TM),
        in_specs=[
            tile,
            pl.BlockSpec((1, 1, 6, d), lambda b, i: (layer, jnp.where(i < nct, nb, b), 0, 0)),
            pl.BlockSpec((1, d), lambda b, i: (0, 0)),
            pl.BlockSpec((d, 5 * d), lambda b, i: (0, 0)),
            pl.BlockSpec(lb_logits.shape, lambda b, i: (0, 0, 0)),
        ],
        out_specs=[tile, pl.BlockSpec((1, d, TM), lambda b, i: (b, 0, i))] + [tile] * 5,
        out_shape=[out(BF16), jax.ShapeDtypeStruct((nb, d, lc), BF16), out(BF16), out(BF16), out(F32), out(F32),
                   out(BF16)],
        compiler_params=_params(("arbitrary", "arbitrary"), VMEM_LIMIT),
        name="hg_prep",
    )(xc, mods, norm_g.reshape(1, d), w_in, lb_logits)


HG_BLOCK = 2 * CHUNK


def _hg_scan_body(qf, vtf, kf, lf, qb, vtb, kb, lb, of, ob, sf, sb):
    @pl.when(pl.program_id(1) == 0)
    def _():
        sf[...] = jnp.zeros_like(sf)
        sb[...] = jnp.zeros_like(sb)

    n, c = HG_BLOCK, CHUNK
    r = lax.broadcasted_iota(I32, (n, n), 0)
    s = lax.broadcasted_iota(I32, (n, n), 1)
    same_chunk = (r < c) == (s < c)
    first = lax.broadcasted_iota(I32, (n, 1), 0) < c
    jobs = []
    for q_ref, vt_ref, k_ref, l_ref, o_ref, s_ref, reverse in (
            (qf, vtf, kf, lf, of, sf, False), (qb, vtb, kb, lb, ob, sb, True)):
        causal = jnp.logical_and(same_chunk, (s >= r) if reverse else (s <= r))
        bc = _dot_exact_lhs(causal.astype(BF16), l_ref[0])
        mid = (c - 1 - c // 2) if reverse else c // 2
        end = 0 if reverse else c - 1
        b_mid = jnp.where(first, bc[mid:mid + 1], bc[c + mid:c + mid + 1])
        b_end = jnp.where(first, bc[end:end + 1], bc[c + end:c + end + 1])
        q = q_ref[0].astype(F32)
        k = k_ref[0].astype(F32)
        vt = vt_ref[0]
        qd = (q * jnp.exp(bc - b_mid)).astype(BF16)
        kd = (k * jnp.exp(b_mid - bc)).astype(BF16)
        qs = q * jnp.exp(bc)
        ke = k * jnp.exp(b_end - bc)
        in_a = jnp.logical_not(first) if reverse else first
        ia, ib = (c, 0) if reverse else (0, c)
        qs_a = jnp.where(in_a, qs, 0.0).astype(BF16)
        qs_b = jnp.where(in_a, 0.0, qs).astype(BF16)
        ke_a = jnp.where(in_a, ke, 0.0).astype(BF16)
        ke_b = jnp.where(in_a, 0.0, ke).astype(BF16)
        dec_a = jnp.exp(bc[ia + end:ia + end + 1])
        dec_b = jnp.exp(bc[ib + end:ib + end + 1])
        dh = q.shape[-1] // HEADS
        for h in range(HEADS):
            sl = slice(h * dh, (h + 1) * dh)
            jobs.append(dict(h=h, sl=sl, o_ref=o_ref, s_ref=s_ref, causal=causal, qd=qd[:, sl], kd=kd[:, sl],
                             vt=vt[sl, :], qs_a=qs_a[:, sl], qs_b=qs_b[:, sl], ke_a=ke_a[:, sl], ke_b=ke_b[:, sl],
                             dec_a=dec_a[:, sl], dec_b=dec_b[:, sl]))
    for j in jobs:
        j["att"] = jnp.where(j["causal"], _dot_nt(j["qd"], j["kd"]), 0.0).astype(BF16)
        j["ua"] = _dot(j["vt"], j["ke_a"])
        j["ub"] = _dot(j["vt"], j["ke_b"])
    for j in jobs:
        s_a = j["s_ref"][j["h"]]
        s_b = s_a * j["dec_a"] + j["ua"]
        j["s_ref"][j["h"]] = s_b * j["dec_b"] + j["ub"]
        j["rhs"] = jnp.concatenate([s_a.astype(BF16), s_b.astype(BF16), j["vt"]], axis=1)
    for j in jobs:
        lhs = jnp.concatenate([j["qs_a"], j["qs_b"], j["att"]], axis=1)
        j["o_ref"][0, :, j["sl"]] = _dot_nt(lhs, j["rhs"])


def _chunk_maps(ncc, nc):
    fwd = lambda b, c: (b, c, 0)
    bwd = lambda b, c: (b, jnp.where(c < ncc, ncc - 1 - c, nc - 1 + ncc - c), 0)
    return fwd, bwd


def _hg_scan(q, vt, kf, kb, lf, lb, ctx_len):
    nb, lc, d = q.shape
    nc = lc // HG_BLOCK
    fwd, bwd = _chunk_maps(ctx_len // HG_BLOCK, nc)
    blk = lambda m: pl.BlockSpec((1, HG_BLOCK, d), m)
    blk_t = lambda m: pl.BlockSpec((1, d, HG_BLOCK), lambda b, c: (m(b, c)[0], 0, m(b, c)[1]))
    ins = lambda m: [blk(m), blk_t(m), blk(m), blk(m)]
    dh = d // HEADS
    return pl.pallas_call(
        _hg_scan_body,
        grid=(nb, nc),
        in_specs=ins(fwd) + ins(bwd),
        out_specs=[blk(fwd), blk(bwd)],
        out_shape=[jax.ShapeDtypeStruct((nb, lc, d), F32)] * 2,
        scratch_shapes=[pltpu.VMEM((HEADS, dh, dh), F32)] * 2,
        compiler_params=_params(("arbitrary", "arbitrary")),
        name="hg_scan",
    )(q, vt, kf, lf, q, vt, kb, lb)


def _post_body(of, ob, g_ref, x_ref, mod_ref, ng_ref, wo_ref, nf_ref, rw_ref, rb_ref,
               x_o, f_o, ti_o, tg_o, cnt_o, *, gate):
    @pl.when(jnp.logical_and(pl.program_id(0) == 0, pl.program_id(1) == 0))
    def _():
        cnt_o[...] = jnp.zeros_like(cnt_o)

    y = of[0] + ob[0]
    dh = y.shape[-1] // HEADS
    parts = []
    for h in range(HEADS):
        yh = y[:, h * dh : (h + 1) * dh]
        parts.append(yh * lax.rsqrt(jnp.mean(yh * yh, axis=-1, keepdims=True) + EPS))
    gt = g_ref[0].astype(F32)
    act = _silu(gt) if gate == "silu" else _sigmoid(gt)
    yn = jnp.concatenate(parts, axis=-1) * ng_ref[...] * act
    mod = mod_ref[0, 0]
    xn = x_ref[0] + mod[2:3] * _dot(yn.astype(BF16), wo_ref[...])
    x_o[0] = xn
    f = _modulate(xn, nf_ref[...], mod[3:4], mod[4:5])
    f_o[0] = f
    vals = _dot_hi(f, rw_ref[...]) + rb_ref[...]
    lane = lax.broadcasted_iota(I32, vals.shape, 1)
    tops, ids = [], []
    chosen = jnp.zeros(vals.shape, F32)
    for _ in range(TOP_K):
        m = jnp.max(vals, axis=-1, keepdims=True)
        idx = jnp.min(jnp.where(vals == m, lane, N_EXPERTS), axis=-1, keepdims=True)
        tops.append(m)
        ids.append(idx)
        hit = lane == idx
        chosen += hit.astype(F32)
        vals = jnp.where(hit, -jnp.inf, vals)
    cnt_o[...] += jnp.sum(chosen, axis=0, keepdims=True)
    es = [jnp.exp(m - tops[0]) for m in tops]
    tot = functools.reduce(lambda a, b: a + b, es)
    tg_o[0] = jnp.concatenate([e / tot for e in es], axis=-1)
    ti_o[0] = jnp.concatenate(ids, axis=-1)


def _post(of, ob, g, xc, mods, norm_g, w_out, norm_ffn_g, router_w, router_b, layer, nct, skip, gate):
    nb, lc, d = xc.shape
    nt = lc // TM - skip
    tile = pl.BlockSpec((1, TM, d), lambda b, i: (b, i + skip, 0))
    otile = pl.BlockSpec((1, TM, d), lambda b, i: (b, i, 0))
    small = pl.BlockSpec((1, TM, TOP_K), lambda b, i: (b, i, 0))
    row = pl.BlockSpec((1, d), lambda b, i: (0, 0))
    return pl.pallas_call(
        functools.partial(_post_body, gate=gate),
        grid=(nb, nt),
        in_specs=[
            tile, tile, tile, tile,
            pl.BlockSpec((1, 1, 6, d), lambda b, i: (layer, jnp.where(i + skip < nct, nb, b), 0, 0)),
            row,
            pl.BlockSpec((d, d), lambda b, i: (0, 0)),
            row,
            pl.BlockSpec((d, N_EXPERTS), lambda b, i: (0, 0)),
            pl.BlockSpec((1, N_EXPERTS), lambda b, i: (0, 0)),
        ],
        out_specs=[tile, otile, small, small, pl.BlockSpec((1, N_EXPERTS), lambda b, i: (0, 0))],
        out_shape=[
            jax.ShapeDtypeStruct((nb, lc, d), F32),
            jax.ShapeDtypeStruct((nb, nt * TM, d), F32),
            jax.ShapeDtypeStruct((nb, nt * TM, TOP_K), I32),
            jax.ShapeDtypeStruct((nb, nt * TM, TOP_K), F32),
            jax.ShapeDtypeStruct((1, N_EXPERTS), F32),
        ],
        input_output_aliases={3: 0},
        compiler_params=_params(("arbitrary", "arbitrary")),
        name="post_mixer",
    )(of, ob, g, xc, mods, norm_g.reshape(1, d), w_out, norm_ffn_g.reshape(1, d), router_w,
      router_b.reshape(1, N_EXPERTS))


def _rank_body(ti_ref, cnt_ref, dest_o, pend_o, carry):
    @pl.when(pl.program_id(0) == 0)
    def _():
        padded = jnp.ceil(cnt_ref[...] * (1.0 / MOE_BLOCK)) * MOE_BLOCK
        r = lax.broadcasted_iota(I32, (N_EXPERTS, N_EXPERTS), 0)
        s = lax.broadcasted_iota(I32, (N_EXPERTS, N_EXPERTS), 1)
        upper = (r <= s).astype(BF16)
        p1, p2, p3 = _split3(jnp.broadcast_to(padded, (8, N_EXPERTS)))
        pends = (_dot(p1, upper) + (_dot(p2, upper) + _dot(p3, upper)))[0:1]
        pend_o[...] = pends
        carry[...] = pends - padded

    ti = ti_ref[...]
    lane = lax.broadcasted_iota(I32, (TM, N_EXPERTS), 1)
    hots = [lane == ti[:, k : k + 1] for k in range(TOP_K)]
    chosen = functools.reduce(lambda a, b: a + b, [h.astype(F32) for h in hots])
    r = lax.broadcasted_iota(I32, (TM, TM), 0)
    s = lax.broadcasted_iota(I32, (TM, TM), 1)
    row = _dot((s < r).astype(BF16), chosen.astype(BF16)) + carry[...]
    dest = [jnp.sum(jnp.where(h, row, 0.0), axis=-1, keepdims=True) for h in hots]
    dest_o[...] = jnp.concatenate(dest, axis=-1).astype(I32)
    carry[...] += jnp.sum(chosen, axis=0, keepdims=True)


def _rank(ti, cnt):
    t = ti.shape[0]
    small = pl.BlockSpec((1, N_EXPERTS), lambda i: (0, 0))
    return pl.pallas_call(
        _rank_body,
        grid=(t // TM,),
        in_specs=[pl.BlockSpec((TM, TOP_K), lambda i: (i, 0)), small],
        out_specs=[pl.BlockSpec((TM, TOP_K), lambda i: (i, 0)), small],
        out_shape=[jax.ShapeDtypeStruct((t, TOP_K), I32), jax.ShapeDtypeStruct((1, N_EXPERTS), F32)],
        scratch_shapes=[pltpu.VMEM((1, N_EXPERTS), F32)],
        compiler_params=_params(("arbitrary",)),
        name="moe_rank",
    )(ti, cnt)


def _row_copy(src, i, dst, j, sem):
    return pltpu.make_async_copy(src.at[pl.ds(i, 1)], dst.at[pl.ds(j, 1)], sem)


def _dispatch_body(ps_ref, cnt_ref, pe_ref, dest_ref, f_ref, xs_hbm, stage, zrow, sem, zsem, *, n_rows, n_pad):
    i = pl.program_id(0)
    last = pl.num_programs(0) - 1

    def wait_tile(slot):
        pltpu.make_async_copy(xs_hbm.at[pl.ds(0, TM * TOP_K)], xs_hbm.at[pl.ds(0, TM * TOP_K)], sem.at[slot]).wait()

    def step(slot):
        @pl.when(i >= 2)
        def _():
            wait_tile(slot)

        stage[slot] = f_ref[...]
        for t in range(TM):
            for k in range(TOP_K):
                _row_copy(stage.at[slot], t, xs_hbm, dest_ref[t * TOP_K + k], sem.at[slot]).start()

        @pl.when(i == last)
        def _():
            wait_tile(slot)

            @pl.when(i >= 1)
            def _():
                wait_tile(1 - slot)

    for slot in range(2):
        pl.when(i % 2 == slot)(functools.partial(step, slot))

    @pl.when(i == 0)
    def _():
        zrow[...] = jnp.zeros_like(zrow)

        def zero(r, carry):
            _row_copy(zrow, 0, xs_hbm, r, zsem).start()
            return carry

        def per_expert(e, carry):
            lax.fori_loop(ps_ref[e] + cnt_ref[e], pe_ref[e], zero, 0)
            return carry

        lax.fori_loop(0, N_EXPERTS, per_expert, 0)
        lax.fori_loop(pe_ref[N_EXPERTS - 1], n_rows, zero, 0)

        def wait_zero(r, carry):
            _row_copy(zrow, 0, xs_hbm, 0, zsem).wait()
            return carry

        lax.fori_loop(0, n_pad, wait_zero, 0)


def _dispatch(pstarts, counts, pends, dest_flat, f, n_rows):
    t, d = f.shape
    n_slots = TM * TOP_K
    return pl.pallas_call(
        functools.partial(_dispatch_body, n_rows=n_rows, n_pad=n_rows - t * TOP_K),
        grid_spec=pltpu.PrefetchScalarGridSpec(
            num_scalar_prefetch=3,
            grid=(t // TM,),
            in_specs=[
                pl.BlockSpec((n_slots,), lambda i, *_: (i,), memory_space=pltpu.SMEM),
                pl.BlockSpec((TM, d), lambda i, *_: (i, 0)),
            ],
            out_specs=pl.BlockSpec(memory_space=pl.ANY),
            scratch_shapes=[pltpu.VMEM((2, TM, d), F32), pltpu.VMEM((8, d), F32), pltpu.SemaphoreType.DMA((2,)),
                            pltpu.SemaphoreType.DMA],
        ),
        out_shape=jax.ShapeDtypeStruct((n_rows, d), F32),
        compiler_params=_params(("arbitrary",)),
        name="moe_dispatch",
    )(pstarts, counts, pends, dest_flat, f)


def _ffn_body(be_ref, nu_ref, xs_ref, w1_ref, b1_ref, w2_ref, b2_ref, ys_ref, w1b, w2b):
    b = pl.program_id(0)
    changed = jnp.logical_or(b == 0, be_ref[b] != be_ref[jnp.maximum(b - 1, 0)])

    @pl.when(changed)
    def _():
        w1b[...] = w1_ref[0].astype(BF16)
        w2b[...] = w2_ref[0].astype(BF16)

    @pl.when(b < nu_ref[0])
    def _():
        u = _dot(xs_ref[...].astype(BF16), w1b[...]) + b1_ref[0]
        dff = u.shape[-1] // 2
        glu = jnp.minimum(u[:, :dff], SWIGLU_LIMIT)
        lin = jnp.clip(u[:, dff:], -SWIGLU_LIMIT, SWIGLU_LIMIT)
        y = glu * _sigmoid(SWIGLU_ALPHA * glu) * (lin + 1.0)
        ys_ref[...] = _dot(y.astype(BF16), w2b[...]) + b2_ref[0]

    @pl.when(b >= nu_ref[0])
    def _():
        ys_ref[...] = jnp.zeros_like(ys_ref)


def _ffn(block_e, n_used, xs, w1, b1, w2, b2, layer):
    n_rows, d = xs.shape
    depth, ne, _, dff2 = w1.shape
    b1 = b1.reshape(depth, ne, 1, dff2)
    b2 = b2.reshape(depth, ne, 1, d)
    wmap = lambda b, be, nu: (layer, be[b], 0, 0)
    return pl.pallas_call(
        _ffn_body,
        grid_spec=pltpu.PrefetchScalarGridSpec(
            num_scalar_prefetch=2,
            grid=(n_rows // MOE_BLOCK,),
            in_specs=[
                pl.BlockSpec((MOE_BLOCK, d), lambda b, be, nu: (b, 0)),
                pl.BlockSpec((None, 1, d, dff2), wmap),
                pl.BlockSpec((None, 1, 1, dff2), wmap),
                pl.BlockSpec((None, 1, dff2 // 2, d), wmap),
                pl.BlockSpec((None, 1, 1, d), wmap),
            ],
            out_specs=pl.BlockSpec((MOE_BLOCK, d), lambda b, be, nu: (b, 0)),
            scratch_shapes=[pltpu.VMEM((d, dff2), BF16), pltpu.VMEM((dff2 // 2, d), BF16)],
        ),
        out_shape=jax.ShapeDtypeStruct((n_rows, d), F32),
        compiler_params=_params(("arbitrary",), VMEM_LIMIT),
        name="moe_ffn",
    )(block_e, n_used, xs, w1, b1, w2, b2)


def _combine_body(dest_ref, next_ref, tg_ref, ys_hbm, x_ref, mod_ref, fg_ref, o_ref, buf, sem, *, final):
    n = pl.program_id(0) * pl.num_programs(1) + pl.program_id(1)
    total = pl.num_programs(0) * pl.num_programs(1)

    def gather(idx_ref, slot):
        for t in range(TM):
            for k in range(TOP_K):
                _row_copy(ys_hbm, idx_ref[t * TOP_K + k], buf.at[slot], k * TM + t, sem.at[slot]).start()

    def step(slot):
        if slot == 0:
            @pl.when(n == 0)
            def _():
                gather(dest_ref, 0)

        @pl.when(n + 1 < total)
        def _():
            gather(next_ref, 1 - slot)

        pltpu.make_async_copy(ys_hbm.at[pl.ds(0, TM * TOP_K)], buf.at[slot], sem.at[slot]).wait()
        g = tg_ref[0]
        acc = g[:, 0:1] * buf[slot, 0:TM]
        for k in range(1, TOP_K):
            acc += g[:, k : k + 1] * buf[slot, k * TM : (k + 1) * TM]
        xn = x_ref[0] + mod_ref[0, 0][5:6] * acc
        o_ref[0] = _rms(xn, fg_ref[...]) if final else xn

    for slot in range(2):
        pl.when(n % 2 == slot)(functools.partial(step, slot))


def _combine(dest_flat, tg, ys, xc, mods, final_g, layer, nct, skip, final):
    nb, lc, d = xc.shape
    nt = lc // TM - skip
    n_slots = TM * TOP_K
    tile = pl.BlockSpec((1, TM, d), lambda b, i: (b, i + skip, 0))
    if final:
        out_spec = pl.BlockSpec((1, TM, d), lambda b, i: (b, i, 0))
        out_shape = jax.ShapeDtypeStruct((nb, nt * TM, d), F32)
        aliases = {}
    else:
        out_spec, out_shape, aliases = tile, jax.ShapeDtypeStruct((nb, lc, d), F32), {4: 0}
    return pl.pallas_call(
        functools.partial(_combine_body, final=final),
        grid=(nb, nt),
        in_specs=[
            pl.BlockSpec((n_slots,), lambda b, i: (b * nt + i,), memory_space=pltpu.SMEM),
            pl.BlockSpec((n_slots,), lambda b, i: (jnp.minimum(b * nt + i + 1, nb * nt - 1),),
                         memory_space=pltpu.SMEM),
            pl.BlockSpec((1, TM, TOP_K), lambda b, i: (b, i, 0)),
            pl.BlockSpec(memory_space=pl.ANY),
            tile,
            pl.BlockSpec((1, 1, 6, d), lambda b, i: (layer, jnp.where(i + skip < nct, nb, b), 0, 0)),
            pl.BlockSpec((1, d), lambda b, i: (0, 0)),
        ],
        out_specs=out_spec,
        scratch_shapes=[pltpu.VMEM((2, n_slots, d), F32), pltpu.SemaphoreType.DMA((2,))],
        out_shape=out_shape,
        input_output_aliases=aliases,
        compiler_params=_params(("arbitrary", "arbitrary"), VMEM_LIMIT),
        name="moe_combine",
    )(dest_flat, dest_flat, tg, ys, xc, mods, final_g.reshape(1, d))


def _moe(f, ti, tg, cnt, xc, mods, w1, b1, w2, b2, final_g, layer, nct, skip, final):
    nb, ntok, d = f.shape
    t = nb * ntok
    dest, pend = _rank(ti.reshape(t, TOP_K), cnt)
    counts = cnt[0].astype(I32)
    pends = pend[0].astype(I32)
    pstarts = pends - (counts + MOE_BLOCK - 1) // MOE_BLOCK * MOE_BLOCK
    n_slots = t * TOP_K
    n_rows = -(-(n_slots + N_EXPERTS * (MOE_BLOCK - 1)) // MOE_BLOCK) * MOE_BLOCK
    n_blocks = n_rows // MOE_BLOCK
    block_start = jnp.arange(n_blocks, dtype=I32) * MOE_BLOCK
    block_e = jnp.minimum(jnp.sum((pends[None, :] <= block_start[:, None]).astype(I32), axis=1), N_EXPERTS - 1)
    n_used = (pends[-1:] // MOE_BLOCK).astype(I32)
    dest_flat = dest.reshape(n_slots)
    xs = _dispatch(pstarts, counts, pends, dest_flat, f.reshape(t, d), n_rows)
    ys = _ffn(block_e, n_used, xs, w1, b1, w2, b2, layer)
    return _combine(dest_flat, tg, ys, xc, mods, final_g, layer, nct, skip, final)


def _ml_prep_body(x_ref, mod_ref, g_ref, wqk_ref, wv_ref, wo_ref, wg_ref, bg_ref,
                  qk_o, v_o, o_o, gi_o, gf_o, git_o, gft_o):
    mod = mod_ref[0, 0]
    h = _modulate(x_ref[0], g_ref[...], mod[0:1], mod[1:2])
    hb = h.astype(BF16)
    qk_o[0] = _dot(hb, wqk_ref[...])
    v_o[0] = _dot(hb, wv_ref[...]).astype(BF16)
    o_o[0] = _dot(hb, wo_ref[...]).astype(BF16)
    gates = GATE_CAP * jnp.tanh((_dot_hi(h, wg_ref[...]) + bg_ref[...]) / GATE_CAP)
    ng = gates.shape[-1] // 2
    fg = gates[:, ng:]
    logf = jnp.minimum(fg, 0.0) - jnp.log1p(jnp.exp(-jnp.abs(fg)))
    pad = jnp.zeros((gates.shape[0], gi_o.shape[-1] - ng), F32)
    gi = jnp.concatenate([gates[:, :ng], pad], axis=-1)
    gf = jnp.concatenate([logf, pad], axis=-1)
    gi_o[0] = gi
    gf_o[0] = gf
    git_o[0] = gi.T[:ng]
    gft_o[0] = gf.T[:ng]


def _ml_prep(xc, mods, norm_g, wqk, wv, wo, wg, bg, layer, nct):
    nb, lc, d = xc.shape
    ng = wg.shape[1] // 2
    tile = lambda w: pl.BlockSpec((1, TM, w), lambda b, i: (b, i, 0))
    full = lambda a: pl.BlockSpec(a.shape, lambda b, i: (0,) * a.ndim)
    return pl.pallas_call(
        _ml_prep_body,
        grid=(nb, lc // TM),
        in_specs=[
            tile(d),
            pl.BlockSpec((1, 1, 6, d), lambda b, i: (layer, jnp.where(i < nct, nb, b), 0, 0)),
            pl.BlockSpec((1, d), lambda b, i: (0, 0)),
            full(wqk), full(wv), full(wo), full(wg), full(bg),
        ],
        out_specs=[tile(wqk.shape[1]), tile(wv.shape[1]), tile(wo.shape[1]), tile(128), tile(128),
                   pl.BlockSpec((1, ng, TM), lambda b, i: (b, 0, i)), pl.BlockSpec((1, ng, TM), lambda b, i: (b, 0, i))],
        out_shape=[
            jax.ShapeDtypeStruct((nb, lc, wqk.shape[1]), F32),
            jax.ShapeDtypeStruct((nb, lc, wv.shape[1]), BF16),
            jax.ShapeDtypeStruct((nb, lc, wo.shape[1]), BF16),
            jax.ShapeDtypeStruct((nb, lc, 128), F32),
            jax.ShapeDtypeStruct((nb, lc, 128), F32),
            jax.ShapeDtypeStruct((nb, ng, lc), F32),
            jax.ShapeDtypeStruct((nb, ng, lc), F32),
        ],
        compiler_params=_params(("arbitrary", "arbitrary"), VMEM_LIMIT),
        name="ml_prep",
    )(xc, mods, norm_g.reshape(1, d), wqk, wv, wo, wg, bg)


CONV_PAD = 72


def _conv_body(main, prev, nxt, cw_ref, cb_ref, q_o, k_o, sc, *, nct, dk):
    i = pl.program_id(1)
    last = pl.num_programs(1) - 1
    p = CONV_PAD
    w = main.shape[-1]
    sc[0:p - GRID_W] = jnp.zeros((p - GRID_W, w), F32)
    sc[p + TM + GRID_W:] = jnp.zeros((sc.shape[0] - p - TM - GRID_W, w), F32)
    sc[p - GRID_W:p] = jnp.where(i > nct, prev[0], 0.0)
    sc[p + TM:p + TM + GRID_W] = jnp.where(jnp.logical_and(i >= nct, i < last), nxt[0], 0.0)
    sc[p:p + TM] = main[0]
    col = lax.broadcasted_iota(I32, (TM, 1), 0) & (GRID_W - 1)

    def finish(acc):
        y = _silu(acc)
        kt = y[:, w // 2:].T
        qs, ks = [], []
        for h in range(HEADS):
            qs += [y[:, h * dk:(h + 1) * dk] * (dk ** -0.5), jnp.zeros((TM, 128 - dk), F32)]
            ks += [kt[h * dk:(h + 1) * dk], jnp.zeros((128 - dk, TM), F32)]
        q_o[0] = jnp.concatenate(qs, axis=1).astype(BF16)
        k_o[0] = jnp.concatenate(ks, axis=0).astype(BF16)

    @pl.when(i < nct)
    def _():
        acc = cb_ref[...] + sc[p:p + TM] * cw_ref[4:5]
        acc += sc[p - 1:p - 1 + TM] * cw_ref[3:4] + sc[p + 1:p + 1 + TM] * cw_ref[5:6]
        finish(acc)

    @pl.when(i >= nct)
    def _():
        acc = jnp.broadcast_to(cb_ref[...], (TM, w))
        for dy in range(3):
            for dx in range(3):
                off = p + (dy - 1) * GRID_W + (dx - 1)
                u = sc[off:off + TM]
                if dx == 0:
                    u = jnp.where(col != 0, u, 0.0)
                if dx == 2:
                    u = jnp.where(col != GRID_W - 1, u, 0.0)
                acc += u * cw_ref[dy * 3 + dx:dy * 3 + dx + 1]
        finish(acc)


def _conv(qk, conv_w, conv_b, nct):
    nb, lc, w = qk.shape
    nt = lc // TM
    per = TM // GRID_W
    nh = lc // GRID_W
    return pl.pallas_call(
        functools.partial(_conv_body, nct=nct, dk=w // 2 // HEADS),
        grid=(nb, nt),
        in_specs=[
            pl.BlockSpec((1, TM, w), lambda b, i: (b, i, 0)),
            pl.BlockSpec((1, GRID_W, w), lambda b, i: (b, jnp.maximum(i * per - 1, 0), 0)),
            pl.BlockSpec((1, GRID_W, w), lambda b, i: (b, jnp.minimum(i * per + per, nh - 1), 0)),
            pl.BlockSpec((9, w), lambda b, i: (0, 0)),
            pl.BlockSpec((1, w), lambda b, i: (0, 0)),
        ],
        out_specs=[pl.BlockSpec((1, TM, HEADS * 128), lambda b, i: (b, i, 0)),
                   pl.BlockSpec((1, HEADS * 128, TM), lambda b, i: (b, 0, i))],
        out_shape=[jax.ShapeDtypeStruct((nb, lc, HEADS * 128), BF16),
                   jax.ShapeDtypeStruct((nb, HEADS * 128, lc), BF16)],
        scratch_shapes=[pltpu.VMEM((CONV_PAD + TM + GRID_W + 8, w), F32)],
        compiler_params=_params(("arbitrary", "arbitrary")),
        name="ml_conv",
    )(qk, qk, qk, conv_w.reshape(9, w), conv_b.reshape(1, w))


ML_CHUNK = 128


def _dot_exact_rhs(x, t):
    p1, p2, p3 = _split3(x)
    return _dot(p1, t) + (_dot(p2, t) + _dot(p3, t))


def _ml_scan_body(qf, ktf, vf, gif, gff, gitf, gftf, qb, ktb, vb, gib, gfb, gitb, gftb,
                  of, ob, c_ref, mrow_ref, mcol_ref):
    c = ML_CHUNK
    nd = 2 * HEADS

    @pl.when(pl.program_id(1) == 0)
    def _():
        for ref in (c_ref, mrow_ref, mcol_ref):
            ref[...] = jnp.zeros_like(ref)

    low = _causal(c, False)
    upp = _causal(c, True)
    low_b, upp_b = low.astype(BF16), upp.astype(BF16)
    neg = -jnp.inf

    lane = lax.broadcasted_iota(I32, (c, 128), 1)
    tpos = lax.broadcasted_iota(I32, (c, 128), 0)
    fwd_l = lane < HEADS
    gi = jnp.where(fwd_l, gif[0], gib[0])
    gf = jnp.where(fwd_l, gff[0], gfb[0])
    bs = jnp.where(fwd_l, _dot_exact_lhs(low_b, gf), _dot_exact_lhs(upp_b, gf))
    a = gi - bs
    pre, suf = a, a
    step = 1
    while step < c:
        pre = jnp.maximum(pre, jnp.where(tpos >= step, pltpu.roll(pre, step, 0), neg))
        suf = jnp.maximum(suf, jnp.where(tpos < c - step, pltpu.roll(suf, c - step, 0), neg))
        step *= 2
    m0 = mrow_ref[0:1]
    mm = jnp.maximum(m0, jnp.where(fwd_l, pre, suf))
    w_inter = jnp.exp(m0 - mm)
    floor = jnp.exp(-(bs + mm))
    fwd_r = fwd_l[0:1]
    b_end = jnp.where(fwd_r, bs[c - 1:c], bs[0:1])
    m_end = jnp.where(fwd_r, mm[c - 1:c], mm[0:1])
    decay = jnp.exp(m0 - m_end)
    mrow_ref[...] = jnp.broadcast_to(b_end + m_end, mrow_ref.shape)

    rowj = lax.broadcasted_iota(I32, (nd, c), 0)
    fwd_t = rowj < HEADS
    git = jnp.where(fwd_t, gitf[0], gitb[0])
    gft = jnp.where(fwd_t, gftf[0], gftb[0])
    bst = jnp.where(fwd_t, _dot_exact_rhs(gft, upp_b), _dot_exact_rhs(gft, low_b))
    at = git - bst
    m0c = mcol_ref[...]
    m_end_c = jnp.maximum(m0c[:, 0:1], jnp.max(at, axis=-1, keepdims=True))
    wk_t = jnp.exp(at - m_end_c)
    b_end_c = jnp.where(fwd_t[:, 0:1], bst[:, c - 1:c], bst[:, 0:1])
    mcol_ref[...] = jnp.broadcast_to(b_end_c + m_end_c, mcol_ref.shape)

    wide = HEADS * 128
    head_of_lane = lax.shift_right_logical(lax.broadcasted_iota(I32, (nd, wide), 1), 7)
    row_w = lax.broadcasted_iota(I32, (nd, wide), 0)
    spos = lax.broadcasted_iota(I32, (c, wide), 1) & 127
    tpos_w = lax.broadcasted_iota(I32, (c, wide), 0)
    pad_rows = jnp.zeros((128 - nd, wide), BF16)
    at_parts = _split3(at)
    mm_parts = _split3(mm)
    ones_blk = jnp.ones((c, 128), BF16)
    dirs = ((0, qf, ktf, vf, of, spos <= tpos_w), (1, qb, ktb, vb, ob, spos >= tpos_w))
    jobs = []
    for d, q_ref, kt_ref, v_ref, o_ref, causal_w in dirs:
        sel = row_w == HEADS * d + head_of_lane
        sel_b = jnp.concatenate([sel.astype(BF16), pad_rows], axis=0)
        tile8 = lambda x: jnp.concatenate([x] * HEADS, axis=1)
        a_rows = [jnp.where(sel, tile8(p), 0).astype(BF16) for p in at_parts]
        zero_rows = jnp.zeros((128 - 3 * nd, wide), BF16)
        rhs = jnp.concatenate(a_rows + [zero_rows, sel_b, sel_b, sel_b], axis=0)
        lhs = jnp.concatenate(
            [(lane < 3 * nd).astype(BF16)] + [(-p).astype(BF16) for p in mm_parts], axis=1)
        w_intra = jnp.where(causal_w, jnp.exp(_dot(lhs, rhs)), 0.0)
        w_inter_w = _dot(w_inter.astype(BF16), sel_b)
        fl_hi, fl_lo = _split2(floor)
        floor_w = _dot(fl_hi, sel_b) + _dot(fl_lo, sel_b)
        dc_hi, dc_lo = _split2(jnp.broadcast_to(decay, (8, 128)))
        decay_w = (_dot(dc_hi, sel_b) + _dot(dc_lo, sel_b))[0:1]
        q = q_ref[0]
        kt = kt_ref[0]
        v = v_ref[0]
        wq = (w_inter_w * q.astype(F32)).astype(BF16)
        for h in range(HEADS):
            sl = slice(h * 128, (h + 1) * 128)
            jobs.append(dict(d=d, h=h, sl=sl, o_ref=o_ref, q=q[:, sl], kt=kt[sl, :], wq=wq[:, sl],
                             w_intra=w_intra[:, sl], floor=floor_w[:, sl], wk=wk_t[HEADS * d + h:HEADS * d + h + 1, :],
                             v_aug=jnp.concatenate([v[:, sl], ones_blk], axis=1),
                             dec=jnp.concatenate([decay_w[:, sl], decay_w[:, sl]], axis=1)))
    for j in jobs:
        j["sc"] = (_dot(j["q"], j["kt"]) * j["w_intra"]).astype(BF16)
    for j in jobs:
        st = c_ref[j["d"], j["h"]]
        out = _dot(jnp.concatenate([j["sc"], j["wq"]], axis=1),
                   jnp.concatenate([j["v_aug"], st.astype(BF16)], axis=0))
        j["o_ref"][0, :, j["sl"]] = out[:, :128] / jnp.maximum(jnp.abs(out[:, 128:]), j["floor"])
    for j in jobs:
        kw_t = (j["kt"].astype(F32) * j["wk"]).astype(BF16)
        c_ref[j["d"], j["h"]] = j["dec"] * c_ref[j["d"], j["h"]] + _dot(kw_t, j["v_aug"])


def _ml_scan(q, kt, v, gi, gf, git, gft, ctx_len):
    nb, lc, wide = q.shape
    dvv = v.shape[-1]
    nc = lc // ML_CHUNK
    fwd, bwd = _chunk_maps(ctx_len // ML_CHUNK, nc)
    swap = lambda m: (lambda b, c: (m(b, c)[0], 0, m(b, c)[1]))
    tm = lambda w, m: pl.BlockSpec((1, ML_CHUNK, w), m)
    ft = lambda r, m: pl.BlockSpec((1, r, ML_CHUNK), swap(m))
    ng = git.shape[1]
    ins = lambda m: [tm(wide, m), ft(wide, m), tm(dvv, m), tm(128, m), tm(128, m), ft(ng, m), ft(ng, m)]
    return pl.pallas_call(
        _ml_scan_body,
        grid=(nb, nc),
        in_specs=ins(fwd) + ins(bwd),
        out_specs=[tm(dvv, fwd), tm(dvv, bwd)],
        out_shape=[jax.ShapeDtypeStruct((nb, lc, dvv), F32)] * 2,
        scratch_shapes=[pltpu.VMEM((2, HEADS, 128, 256), F32), pltpu.VMEM((8, 128), F32),
                        pltpu.VMEM((ng, 128), F32)],
        compiler_params=_params(("arbitrary", "arbitrary"), VMEM_LIMIT),
        name="ml_scan",
    )(q, kt, v, gi, gf, git, gft, q, kt, v, gi, gf, git, gft)


def kernel(x, c, ctx, c_ctx, ada_w, ada_b, norm_mix_g, norm_ffn_g, hg_w_in, hg_lb_logits, hg_norm_g, hg_w_out,
           ml_w_in, ml_b_gate, ml_conv_w, ml_conv_b, ml_norm_g, ml_w_out, router_w, router_b,
           moe_w1, moe_b1, moe_w2, moe_b2, final_norm_g):
    nb, seq, d = x.shape
    ctx_len = ctx.shape[1]
    depth = ada_w.shape[0]
    assert ctx_len == TM and seq % TM == 0 and seq % GRID_W == 0 and d % (HEADS * 128) == 0
    nct = ctx_len // TM

    xc = jnp.concatenate([ctx, x], axis=1)
    rows = -(-(nb + 1) // 8) * 8
    cstack = jnp.concatenate([c, c_ctx[None, :], jnp.zeros((rows - nb - 1, d), F32)], axis=0)
    mods = _ada(cstack, ada_w, ada_b).reshape(depth, rows, 6, d)

    for layer in range(depth):
        j = layer // 2
        last = layer == depth - 1
        if layer % 2 == 0:
            q, vt, kf, kb, lf, lb, g = _hg_prep(xc, mods, norm_mix_g[layer], hg_w_in[j].astype(BF16),
                                                hg_lb_logits, layer, nct)
            of, ob = _hg_scan(q, vt, kf, kb, lf, lb, ctx_len)
            norm_g, w_out, gate = hg_norm_g[j], hg_w_out[j], "silu"
        else:
            w = ml_w_in[j]
            dqk = ml_conv_w.shape[-1]
            dv = ml_w_out.shape[1]
            qk, v, g, gi, gf, git, gft = _ml_prep(
                xc, mods, norm_mix_g[layer], w[:, :dqk].astype(BF16), w[:, dqk:dqk + dv].astype(BF16),
                w[:, dqk + dv:dqk + 2 * dv].astype(BF16), w[:, dqk + 2 * dv:], ml_b_gate[j][None, :], layer, nct)
            q, kt = _conv(qk, ml_conv_w[j], ml_conv_b[j], nct)
            of, ob = _ml_scan(q, kt, v, gi, gf, git, gft, ctx_len)
            norm_g, w_out, gate = ml_norm_g[j], ml_w_out[j], "sigmoid"
        skip = nct if last else 0
        xc, f, ti, tg, cnt = _post(of, ob, g, xc, mods, norm_g, w_out.astype(BF16), norm_ffn_g[layer],
                                   router_w[layer], router_b[layer], layer, nct, skip, gate)
        xc = _moe(f, ti, tg, cnt, xc, mods, moe_w1, moe_b1, moe_w2, moe_b2, final_norm_g, layer, nct, skip, last)
    return xc
```

```python
import functools

import jax
import jax.numpy as jnp
from jax import lax
from jax.experimental import pallas as pl
from jax.experimental.pallas import tpu as pltpu

F32 = jnp.float32
BF16 = jnp.bfloat16
I32 = jnp.int32

EPS = 1e-6
HEADS = 8
CHUNK = 64
GATE_CAP = 15.0
GRID_W = 64
N_EXPERTS = 32
TOP_K = 4
MOE_BLOCK = 256
SWIGLU_LIMIT = 7.0
SWIGLU_ALPHA = 1.702

TM = 256
ADA_TN = 512
VMEM_LIMIT = 56 * 1024 * 1024


def _params(sem, vmem=None):
    return pltpu.CompilerParams(dimension_semantics=sem, vmem_limit_bytes=vmem)


def _dot(a, b):
    return jnp.dot(a, b, preferred_element_type=F32)


def _dot_nt(a, b):
    return lax.dot_general(a, b, (((1,), (1,)), ((), ())), preferred_element_type=F32)


def _dot_tn(a, b):
    return lax.dot_general(a, b, (((0,), (0,)), ((), ())), preferred_element_type=F32)


def _split2(x):
    hi = x.astype(BF16)
    lo = (x - hi.astype(F32)).astype(BF16)
    return hi, lo


def _split3(x):
    p1 = x.astype(BF16)
    r1 = x - p1.astype(F32)
    p2 = r1.astype(BF16)
    p3 = (r1 - p2.astype(F32)).astype(BF16)
    return p1, p2, p3


def _dot_hi(a, w):
    a1, a2 = _split2(a)
    w1, w2 = _split2(w)
    return _dot(a1, w1) + (_dot(a1, w2) + _dot(a2, w1))


def _dot_exact_lhs(t, x):
    p1, p2, p3 = _split3(x)
    return _dot(t, p1) + (_dot(t, p2) + _dot(t, p3))


def _sigmoid(x):
    return jax.nn.sigmoid(x)


def _silu(x):
    return x * jax.nn.sigmoid(x)


def _rms(x, g):
    return x * lax.rsqrt(jnp.mean(x * x, axis=-1, keepdims=True) + EPS) * g


def _modulate(x, g, shift, scale):
    return _rms(x, g) * (1.0 + scale) + shift


def _causal(n, reverse):
    r = lax.broadcasted_iota(I32, (n, n), 0)
    s = lax.broadcasted_iota(I32, (n, n), 1)
    return (s >= r) if reverse else (s <= r)


def _ada_body(c_ref, w_ref, b_ref, o_ref):
    a = _silu(c_ref[...])
    o_ref[0] = _dot_hi(a, w_ref[0]) + b_ref[0]


def _ada(cstack, ada_w, ada_b):
    depth, d, n6 = ada_w.shape
    nb = cstack.shape[0]
    return pl.pallas_call(
        _ada_body,
        grid=(depth, n6 // ADA_TN),
        in_specs=[
            pl.BlockSpec((nb, d), lambda l, j: (0, 0)),
            pl.BlockSpec((1, d, ADA_TN), lambda l, j: (l, 0, j)),
            pl.BlockSpec((1, 1, ADA_TN), lambda l, j: (l, 0, j)),
        ],
        out_specs=pl.BlockSpec((1, nb, ADA_TN), lambda l, j: (l, 0, j)),
        out_shape=jax.ShapeDtypeStruct((depth, nb, n6), F32),
        compiler_params=_params(("arbitrary", "arbitrary")),
        name="ada",
    )(cstack, ada_w, ada_b.reshape(depth, 1, n6))


def _hg_prep_body(x_ref, mod_ref, g_ref, w_ref, lb_ref, q_o, v_o, kf_o, kb_o, lf_o, lbw_o, g_o, *, layer, d):
    mod = mod_ref[0, 0]
    h = _modulate(x_ref[0], g_ref[...], mod[0:1], mod[1:2]).astype(BF16)

    def lower_bound(direction):
        rows = [lb_ref[direction, r : r + 1, :] for r in range(lb_ref.shape[1])]
        mx = functools.reduce(jnp.maximum, rows)
        es = [jnp.exp(r - mx) for r in rows]
        return sum(es[: layer + 1]) / sum(es)

    def proj(j):
        return _dot(h, w_ref[:, j * d : (j + 1) * d])

    q_o[0] = _silu(proj(0)).astype(BF16)
    v_o[0] = proj(1).T.astype(BF16)
    for j, k_o, l_o in ((2, kf_o, lf_o), (3, kb_o, lbw_o)):
        lb = lower_bound(j - 2)
        f = lb + (1.0 - lb) * _sigmoid(proj(j))
        k_o[0] = (1.0 - f).astype(BF16)
        l_o[0] = jnp.log(f)
    g_o[0] = proj(4).astype(BF16)


def _hg_prep(xc, mods, norm_g, w_in, lb_logits, layer, nct):
    nb, lc, d = xc.shape
    tile = pl.BlockSpec((1, TM, d), lambda b, i: (b, i, 0))
    out = lambda dt: jax.ShapeDtypeStruct((nb, lc, d), dt)
    return pl.pallas_call(
        functools.partial(_hg_prep_body, layer=layer, d=d),
        grid=(nb, lc // TM),
        in_specs=[
            tile,
            pl.BlockSpec((1, 1, 6, d), lambda b, i: (layer, jnp.where(i < nct, nb, b), 0, 0)),
            pl.BlockSpec((1, d), lambda b, i: (0, 0)),
            pl.BlockSpec((d, 5 * d), lambda b, i: (0, 0)),
            pl.BlockSpec(lb_logits.shape, lambda b, i: (0, 0, 0)),
        ],
        out_specs=[tile, pl.BlockSpec((1, d, TM), lambda b, i: (b, 0, i))] + [tile] * 5,
        out_shape=[out(BF16), jax.ShapeDtypeStruct((nb, d, lc), BF16), out(BF16), out(BF16), out(F32), out(F32),
                   out(BF16)],
        compiler_params=_params(("arbitrary", "arbitrary"), VMEM_LIMIT),
        name="hg_prep",
    )(xc, mods, norm_g.reshape(1, d), w_in, lb_logits)


HG_BLOCK = 2 * CHUNK


def _hg_scan_body(qf, vtf, kf, lf, qb, vtb, kb, lb, of, ob, sf, sb):
    @pl.when(pl.program_id(1) == 0)
    def _():
        sf[...] = jnp.zeros_like(sf)
        sb[...] = jnp.zeros_like(sb)

    n, c = HG_BLOCK, CHUNK
    r = lax.broadcasted_iota(I32, (n, n), 0)
    s = lax.broadcasted_iota(I32, (n, n), 1)
    same_chunk = (r < c) == (s < c)
    first = lax.broadcasted_iota(I32, (n, 1), 0) < c
    jobs = []
    for q_ref, vt_ref, k_ref, l_ref, o_ref, s_ref, reverse in (
            (qf, vtf, kf, lf, of, sf, False), (qb, vtb, kb, lb, ob, sb, True)):
        causal = jnp.logical_and(same_chunk, (s >= r) if reverse else (s <= r))
        bc = _dot_exact_lhs(causal.astype(BF16), l_ref[0])
        mid = (c - 1 - c // 2) if reverse else c // 2
        end = 0 if reverse else c - 1
        b_mid = jnp.where(first, bc[mid:mid + 1], bc[c + mid:c + mid + 1])
        b_end = jnp.where(first, bc[end:end + 1], bc[c + end:c + end + 1])
        q = q_ref[0].astype(F32)
        k = k_ref[0].astype(F32)
        vt = vt_ref[0]
        qd = (q * jnp.exp(bc - b_mid)).astype(BF16)
        kd = (k * jnp.exp(b_mid - bc)).astype(BF16)
        qs = q * jnp.exp(bc)
        ke = k * jnp.exp(b_end - bc)
        in_a = jnp.logical_not(first) if reverse else first
        ia, ib = (c, 0) if reverse else (0, c)
        qs_a = jnp.where(in_a, qs, 0.0).astype(BF16)
        qs_b = jnp.where(in_a, 0.0, qs).astype(BF16)
        ke_a = jnp.where(in_a, ke, 0.0).astype(BF16)
        ke_b = jnp.where(in_a, 0.0, ke).astype(BF16)
        dec_a = jnp.exp(bc[ia + end:ia + end + 1])
        dec_b = jnp.exp(bc[ib + end:ib + end + 1])
        dh = q.shape[-1] // HEADS
        for h in range(HEADS):
            sl = slice(h * dh, (h + 1) * dh)
            jobs.append(dict(h=h, sl=sl, o_ref=o_ref, s_ref=s_ref, causal=causal, qd=qd[:, sl], kd=kd[:, sl],
                             vt=vt[sl, :], qs_a=qs_a[:, sl], qs_b=qs_b[:, sl], ke_a=ke_a[:, sl], ke_b=ke_b[:, sl],
                             dec_a=dec_a[:, sl], dec_b=dec_b[:, sl]))
    for j in jobs:
        j["att"] = jnp.where(j["causal"], _dot_nt(j["qd"], j["kd"]), 0.0).astype(BF16)
        j["ua"] = _dot(j["vt"], j["ke_a"])
        j["ub"] = _dot(j["vt"], j["ke_b"])
    for j in jobs:
        s_a = j["s_ref"][j["h"]]
        s_b = s_a * j["dec_a"] + j["ua"]
        j["s_ref"][j["h"]] = s_b * j["dec_b"] + j["ub"]
        j["rhs"] = jnp.concatenate([s_a.astype(BF16), s_b.astype(BF16), j["vt"]], axis=1)
    for j in jobs:
        lhs = jnp.concatenate([j["qs_a"], j["qs_b"], j["att"]], axis=1)
        j["o_ref"][0, :, j["sl"]] = _dot_nt(lhs, j["rhs"])


def _chunk_maps(ncc, nc):
    fwd = lambda b, c: (b, c, 0)
    bwd = lambda b, c: (b, jnp.where(c < ncc, ncc - 1 - c, nc - 1 + ncc - c), 0)
    return fwd, bwd


def _hg_scan(q, vt, kf, kb, lf, lb, ctx_len):
    nb, lc, d = q.shape
    nc = lc // HG_BLOCK
    fwd, bwd = _chunk_maps(ctx_len // HG_BLOCK, nc)
    blk = lambda m: pl.BlockSpec((1, HG_BLOCK, d), m)
    blk_t = lambda m: pl.BlockSpec((1, d, HG_BLOCK), lambda b, c: (m(b, c)[0], 0, m(b, c)[1]))
    ins = lambda m: [blk(m), blk_t(m), blk(m), blk(m)]
    dh = d // HEADS
    return pl.pallas_call(
        _hg_scan_body,
        grid=(nb, nc),
        in_specs=ins(fwd) + ins(bwd),
        out_specs=[blk(fwd), blk(bwd)],
        out_shape=[jax.ShapeDtypeStruct((nb, lc, d), F32)] * 2,
        scratch_shapes=[pltpu.VMEM((HEADS, dh, dh), F32)] * 2,
        compiler_params=_params(("arbitrary", "arbitrary")),
        name="hg_scan",
    )(q, vt, kf, lf, q, vt, kb, lb)


def _post_body(of, ob, g_ref, x_ref, mod_ref, ng_ref, wo_ref, nf_ref, rw_ref, rb_ref,
               x_o, f_o, ti_o, tg_o, cnt_o, *, gate):
    @pl.when(jnp.logical_and(pl.program_id(0) == 0, pl.program_id(1) == 0))
    def _():
        cnt_o[...] = jnp.zeros_like(cnt_o)

    y = of[0] + ob[0]
    dh = y.shape[-1] // HEADS
    parts = []
    for h in range(HEADS):
        yh = y[:, h * dh : (h + 1) * dh]
        parts.append(yh * lax.rsqrt(jnp.mean(yh * yh, axis=-1, keepdims=True) + EPS))
    gt = g_ref[0].astype(F32)
    act = _silu(gt) if gate == "silu" else _sigmoid(gt)
    yn = jnp.concatenate(parts, axis=-1) * ng_ref[...] * act
    mod = mod_ref[0, 0]
    xn = x_ref[0] + mod[2:3] * _dot(yn.astype(BF16), wo_ref[...])
    x_o[0] = xn
    f = _modulate(xn, nf_ref[...], mod[3:4], mod[4:5])
    f_o[0] = f
    vals = _dot_hi(f, rw_ref[...]) + rb_ref[...]
    lane = lax.broadcasted_iota(I32, vals.shape, 1)
    tops, ids = [], []
    chosen = jnp.zeros(vals.shape, F32)
    for _ in range(TOP_K):
        m = jnp.max(vals, axis=-1, keepdims=True)
        idx = jnp.min(jnp.where(vals == m, lane, N_EXPERTS), axis=-1, keepdims=True)
        tops.append(m)
        ids.append(idx)
        hit = lane == idx
        chosen += hit.astype(F32)
        vals = jnp.where(hit, -jnp.inf, vals)
    cnt_o[...] += jnp.sum(chosen, axis=0, keepdims=True)
    es = [jnp.exp(m - tops[0]) for m in tops]
    tot = functools.reduce(lambda a, b: a + b, es)
    tg_o[0] = jnp.concatenate([e / tot for e in es], axis=-1)
    ti_o[0] = jnp.concatenate(ids, axis=-1)


def _post(of, ob, g, xc, mods, norm_g, w_out, norm_ffn_g, router_w, router_b, layer, nct, skip, gate):
    nb, lc, d = xc.shape
    nt = lc // TM - skip
    tile = pl.BlockSpec((1, TM, d), lambda b, i: (b, i + skip, 0))
    otile = pl.BlockSpec((1, TM, d), lambda b, i: (b, i, 0))
    small = pl.BlockSpec((1, TM, TOP_K), lambda b, i: (b, i, 0))
    row = pl.BlockSpec((1, d), lambda b, i: (0, 0))
    return pl.pallas_call(
        functools.partial(_post_body, gate=gate),
        grid=(nb, nt),
        in_specs=[
            tile, tile, tile, tile,
            pl.BlockSpec((1, 1, 6, d), lambda b, i: (layer, jnp.where(i + skip < nct, nb, b), 0, 0)),
            row,
            pl.BlockSpec((d, d), lambda b, i: (0, 0)),
            row,
            pl.BlockSpec((d, N_EXPERTS), lambda b, i: (0, 0)),
            pl.BlockSpec((1, N_EXPERTS), lambda b, i: (0, 0)),
        ],
        out_specs=[tile, otile, small, small, pl.BlockSpec((1, N_EXPERTS), lambda b, i: (0, 0))],
        out_shape=[
            jax.ShapeDtypeStruct((nb, lc, d), F32),
            jax.ShapeDtypeStruct((nb, nt * TM, d), F32),
            jax.ShapeDtypeStruct((nb, nt * TM, TOP_K), I32),
            jax.ShapeDtypeStruct((nb, nt * TM, TOP_K), F32),
            jax.ShapeDtypeStruct((1, N_EXPERTS), F32),
        ],
        input_output_aliases={3: 0},
        compiler_params=_params(("arbitrary", "arbitrary")),
        name="post_mixer",
    )(of, ob, g, xc, mods, norm_g.reshape(1, d), w_out, norm_ffn_g.reshape(1, d), router_w,
      router_b.reshape(1, N_EXPERTS))


def _rank_body(ti_ref, cnt_ref, dest_o, pend_o, carry):
    @pl.when(pl.program_id(0) == 0)
    def _():
        padded = jnp.ceil(cnt_ref[...] * (1.0 / MOE_BLOCK)) * MOE_BLOCK
        r = lax.broadcasted_iota(I32, (N_EXPERTS, N_EXPERTS), 0)
        s = lax.broadcasted_iota(I32, (N_EXPERTS, N_EXPERTS), 1)
        upper = (r <= s).astype(BF16)
        p1, p2, p3 = _split3(jnp.broadcast_to(padded, (8, N_EXPERTS)))
        pends = (_dot(p1, upper) + (_dot(p2, upper) + _dot(p3, upper)))[0:1]
        pend_o[...] = pends
        carry[...] = pends - padded

    ti = ti_ref[...]
    lane = lax.broadcasted_iota(I32, (TM, N_EXPERTS), 1)
    hots = [lane == ti[:, k : k + 1] for k in range(TOP_K)]
    chosen = functools.reduce(lambda a, b: a + b, [h.astype(F32) for h in hots])
    r = lax.broadcasted_iota(I32, (TM, TM), 0)
    s = lax.broadcasted_iota(I32, (TM, TM), 1)
    row = _dot((s < r).astype(BF16), chosen.astype(BF16)) + carry[...]
    dest = [jnp.sum(jnp.where(h, row, 0.0), axis=-1, keepdims=True) for h in hots]
    dest_o[...] = jnp.concatenate(dest, axis=-1).astype(I32)
    carry[...] += jnp.sum(chosen, axis=0, keepdims=True)


def _rank(ti, cnt):
    t = ti.shape[0]
    small = pl.BlockSpec((1, N_EXPERTS), lambda i: (0, 0))
    return pl.pallas_call(
        _rank_body,
        grid=(t // TM,),
        in_specs=[pl.BlockSpec((TM, TOP_K), lambda i: (i, 0)), small],
        out_specs=[pl.BlockSpec((TM, TOP_K), lambda i: (i, 0)), small],
        out_shape=[jax.ShapeDtypeStruct((t, TOP_K), I32), jax.ShapeDtypeStruct((1, N_EXPERTS), F32)],
        scratch_shapes=[pltpu.VMEM((1, N_EXPERTS), F32)],
        compiler_params=_params(("arbitrary",)),
        name="moe_rank",
    )(ti, cnt)


def _row_copy(src, i, dst, j, sem):
    return pltpu.make_async_copy(src.at[pl.ds(i, 1)], dst.at[pl.ds(j, 1)], sem)


def _dispatch_body(ps_ref, cnt_ref, pe_ref, dest_ref, f_ref, xs_hbm, stage, zrow, sem, zsem, *, n_rows, n_pad):
    i = pl.program_id(0)
    last = pl.num_programs(0) - 1

    def wait_tile(slot):
        pltpu.make_async_copy(xs_hbm.at[pl.ds(0, TM * TOP_K)], xs_hbm.at[pl.ds(0, TM * TOP_K)], sem.at[slot]).wait()

    def step(slot):
        @pl.when(i >= 2)
        def _():
            wait_tile(slot)

        stage[slot] = f_ref[...]
        for t in range(TM):
            for k in range(TOP_K):
                _row_copy(stage.at[slot], t, xs_hbm, dest_ref[t * TOP_K + k], sem.at[slot]).start(priority=k % 2)

        @pl.when(i == last)
        def _():
            wait_tile(slot)

            @pl.when(i >= 1)
            def _():
                wait_tile(1 - slot)

    for slot in range(2):
        pl.when(i % 2 == slot)(functools.partial(step, slot))

    @pl.when(i == 0)
    def _():
        zrow[...] = jnp.zeros_like(zrow)

        def zero(r, carry):
            _row_copy(zrow, 0, xs_hbm, r, zsem).start()
            return carry

        def per_expert(e, carry):
            lax.fori_loop(ps_ref[e] + cnt_ref[e], pe_ref[e], zero, 0)
            return carry

        lax.fori_loop(0, N_EXPERTS, per_expert, 0)
        lax.fori_loop(pe_ref[N_EXPERTS - 1], n_rows, zero, 0)

        def wait_zero(r, carry):
            _row_copy(zrow, 0, xs_hbm, 0, zsem).wait()
            return carry

        lax.fori_loop(0, n_pad, wait_zero, 0)


def _dispatch(pstarts, counts, pends, dest_flat, f, n_rows):
    t, d = f.shape
    n_slots = TM * TOP_K
    return pl.pallas_call(
        functools.partial(_dispatch_body, n_rows=n_rows, n_pad=n_rows - t * TOP_K),
        grid_spec=pltpu.PrefetchScalarGridSpec(
            num_scalar_prefetch=3,
            grid=(t // TM,),
            in_specs=[
                pl.BlockSpec((n_slots,), lambda i, *_: (i,), memory_space=pltpu.SMEM),
                pl.BlockSpec((TM, d), lambda i, *_: (i, 0)),
            ],
            out_specs=pl.BlockSpec(memory_space=pl.ANY),
            scratch_shapes=[pltpu.VMEM((2, TM, d), F32), pltpu.VMEM((8, d), F32), pltpu.SemaphoreType.DMA((2,)),
                            pltpu.SemaphoreType.DMA],
        ),
        out_shape=jax.ShapeDtypeStruct((n_rows, d), F32),
        compiler_params=_params(("arbitrary",)),
        name="moe_dispatch",
    )(pstarts, counts, pends, dest_flat, f)


def _ffn_body(be_ref, nu_ref, xs_ref, w1_ref, b1_ref, w2_ref, b2_ref, ys_ref, w1b, w2b):
    b = pl.program_id(0)
    changed = jnp.logical_or(b == 0, be_ref[b] != be_ref[jnp.maximum(b - 1, 0)])

    @pl.when(changed)
    def _():
        w1b[...] = w1_ref[0].astype(BF16)
        w2b[...] = w2_ref[0].astype(BF16)

    @pl.when(b < nu_ref[0])
    def _():
        u = _dot(xs_ref[...].astype(BF16), w1b[...]) + b1_ref[0]
        dff = u.shape[-1] // 2
        glu = jnp.minimum(u[:, :dff], SWIGLU_LIMIT)
        lin = jnp.clip(u[:, dff:], -SWIGLU_LIMIT, SWIGLU_LIMIT)
        y = glu * _sigmoid(SWIGLU_ALPHA * glu) * (lin + 1.0)
        ys_ref[...] = _dot(y.astype(BF16), w2b[...]) + b2_ref[0]

    @pl.when(b >= nu_ref[0])
    def _():
        ys_ref[...] = jnp.zeros_like(ys_ref)


def _ffn(block_e, n_used, xs, w1, b1, w2, b2, layer):
    n_rows, d = xs.shape
    depth, ne, _, dff2 = w1.shape
    b1 = b1.reshape(depth, ne, 1, dff2)
    b2 = b2.reshape(depth, ne, 1, d)
    wmap = lambda b, be, nu: (layer, be[b], 0, 0)
    return pl.pallas_call(
        _ffn_body,
        grid_spec=pltpu.PrefetchScalarGridSpec(
            num_scalar_prefetch=2,
            grid=(n_rows // MOE_BLOCK,),
            in_specs=[
                pl.BlockSpec((MOE_BLOCK, d), lambda b, be, nu: (b, 0)),
                pl.BlockSpec((None, 1, d, dff2), wmap),
                pl.BlockSpec((None, 1, 1, dff2), wmap),
                pl.BlockSpec((None, 1, dff2 // 2, d), wmap),
                pl.BlockSpec((None, 1, 1, d), wmap),
            ],
            out_specs=pl.BlockSpec((MOE_BLOCK, d), lambda b, be, nu: (b, 0)),
            scratch_shapes=[pltpu.VMEM((d, dff2), BF16), pltpu.VMEM((dff2 // 2, d), BF16)],
        ),
        out_shape=jax.ShapeDtypeStruct((n_rows, d), F32),
        compiler_params=_params(("arbitrary",), VMEM_LIMIT),
        name="moe_ffn",
    )(block_e, n_used, xs, w1, b1, w2, b2)


def _combine_body(dest_ref, next_ref, tg_ref, ys_hbm, x_ref, mod_ref, fg_ref, o_ref, buf, sem, *, final):
    n = pl.program_id(0) * pl.num_programs(1) + pl.program_id(1)
    total = pl.num_programs(0) * pl.num_programs(1)

    def gather(idx_ref, slot):
        for t in range(TM):
            for k in range(TOP_K):
                _row_copy(ys_hbm, idx_ref[t * TOP_K + k], buf.at[slot], k * TM + t, sem.at[slot]).start(priority=k % 2)

    def step(slot):
        if slot == 0:
            @pl.when(n == 0)
            def _():
                gather(dest_ref, 0)

        @pl.when(n + 1 < total)
        def _():
            gather(next_ref, 1 - slot)

        pltpu.make_async_copy(ys_hbm.at[pl.ds(0, TM * TOP_K)], buf.at[slot], sem.at[slot]).wait()
        g = tg_ref[0]
        acc = g[:, 0:1] * buf[slot, 0:TM]
        for k in range(1, TOP_K):
            acc += g[:, k : k + 1] * buf[slot, k * TM : (k + 1) * TM]
        xn = x_ref[0] + mod_ref[0, 0][5:6] * acc
        o_ref[0] = _rms(xn, fg_ref[...]) if final else xn

    for slot in range(2):
        pl.when(n % 2 == slot)(functools.partial(step, slot))


def _combine(dest_flat, tg, ys, xc, mods, final_g, layer, nct, skip, final):
    nb, lc, d = xc.shape
    nt = lc // TM - skip
    n_slots = TM * TOP_K
    tile = pl.BlockSpec((1, TM, d), lambda b, i: (b, i + skip, 0))
    if final:
        out_spec = pl.BlockSpec((1, TM, d), lambda b, i: (b, i, 0))
        out_shape = jax.ShapeDtypeStruct((nb, nt * TM, d), F32)
        aliases = {}
    else:
        out_spec, out_shape, aliases = tile, jax.ShapeDtypeStruct((nb, lc, d), F32), {4: 0}
    return pl.pallas_call(
        functools.partial(_combine_body, final=final),
        grid=(nb, nt),
        in_specs=[
            pl.BlockSpec((n_slots,), lambda b, i: (b * nt + i,), memory_space=pltpu.SMEM),
            pl.BlockSpec((n_slots,), lambda b, i: (jnp.minimum(b * nt + i + 1, nb * nt - 1),),
                         memory_space=pltpu.SMEM),
            pl.BlockSpec((1, TM, TOP_K), lambda b, i: (b, i, 0)),
            pl.BlockSpec(memory_space=pl.ANY),
            tile,
            pl.BlockSpec((1, 1, 6, d), lambda b, i: (layer, jnp.where(i + skip < nct, nb, b), 0, 0)),
            pl.BlockSpec((1, d), lambda b, i: (0, 0)),
        ],
        out_specs=out_spec,
        scratch_shapes=[pltpu.VMEM((2, n_slots, d), F32), pltpu.SemaphoreType.DMA((2,))],
        out_shape=out_shape,
        input_output_aliases=aliases,
        compiler_params=_params(("arbitrary", "arbitrary"), VMEM_LIMIT),
        name="moe_combine",
    )(dest_flat, dest_flat, tg, ys, xc, mods, final_g.reshape(1, d))


def _moe(f, ti, tg, cnt, xc, mods, w1, b1, w2, b2, final_g, layer, nct, skip, final):
    nb, ntok, d = f.shape
    t = nb * ntok
    dest, pend = _rank(ti.reshape(t, TOP_K), cnt)
    counts = cnt[0].astype(I32)
    pends = pend[0].astype(I32)
    pstarts = pends - (counts + MOE_BLOCK - 1) // MOE_BLOCK * MOE_BLOCK
    n_slots = t * TOP_K
    n_rows = -(-(n_slots + N_EXPERTS * (MOE_BLOCK - 1)) // MOE_BLOCK) * MOE_BLOCK
    n_blocks = n_rows // MOE_BLOCK
    block_start = jnp.arange(n_blocks, dtype=I32) * MOE_BLOCK
    block_e = jnp.minimum(jnp.sum((pends[None, :] <= block_start[:, None]).astype(I32), axis=1), N_EXPERTS - 1)
    n_used = (pends[-1:] // MOE_BLOCK).astype(I32)
    dest_flat = dest.reshape(n_slots)
    xs = _dispatch(pstarts, counts, pends, dest_flat, f.reshape(t, d), n_rows)
    ys = _ffn(block_e, n_used, xs, w1, b1, w2, b2, layer)
    return _combine(dest_flat, tg, ys, xc, mods, final_g, layer, nct, skip, final)


def _ml_prep_body(x_ref, mod_ref, g_ref, wqk_ref, wv_ref, wo_ref, wg_ref, bg_ref,
                  qk_o, v_o, o_o, gi_o, gf_o, git_o, gft_o):
    mod = mod_ref[0, 0]
    h = _modulate(x_ref[0], g_ref[...], mod[0:1], mod[1:2])
    hb = h.astype(BF16)
    qk_o[0] = _dot(hb, wqk_ref[...])
    v_o[0] = _dot(hb, wv_ref[...]).astype(BF16)
    o_o[0] = _dot(hb, wo_ref[...]).astype(BF16)
    gates = GATE_CAP * jnp.tanh((_dot_hi(h, wg_ref[...]) + bg_ref[...]) / GATE_CAP)
    ng = gates.shape[-1] // 2
    fg = gates[:, ng:]
    logf = jnp.minimum(fg, 0.0) - jnp.log1p(jnp.exp(-jnp.abs(fg)))
    pad = jnp.zeros((gates.shape[0], gi_o.shape[-1] - ng), F32)
    gi = jnp.concatenate([gates[:, :ng], pad], axis=-1)
    gf = jnp.concatenate([logf, pad], axis=-1)
    gi_o[0] = gi
    gf_o[0] = gf
    git_o[0] = gi.T[:ng]
    gft_o[0] = gf.T[:ng]


def _ml_prep(xc, mods, norm_g, wqk, wv, wo, wg, bg, layer, nct):
    nb, lc, d = xc.shape
    ng = wg.shape[1] // 2
    tile = lambda w: pl.BlockSpec((1, TM, w), lambda b, i: (b, i, 0))
    full = lambda a: pl.BlockSpec(a.shape, lambda b, i: (0,) * a.ndim)
    return pl.pallas_call(
        _ml_prep_body,
        grid=(nb, lc // TM),
        in_specs=[
            tile(d),
            pl.BlockSpec((1, 1, 6, d), lambda b, i: (layer, jnp.where(i < nct, nb, b), 0, 0)),
            pl.BlockSpec((1, d), lambda b, i: (0, 0)),
            full(wqk), full(wv), full(wo), full(wg), full(bg),
        ],
        out_specs=[tile(wqk.shape[1]), tile(wv.shape[1]), tile(wo.shape[1]), tile(128), tile(128),
                   pl.BlockSpec((1, ng, TM), lambda b, i: (b, 0, i)), pl.BlockSpec((1, ng, TM), lambda b, i: (b, 0, i))],
        out_shape=[
            jax.ShapeDtypeStruct((nb, lc, wqk.shape[1]), F32),
            jax.ShapeDtypeStruct((nb, lc, wv.shape[1]), BF16),
            jax.ShapeDtypeStruct((nb, lc, wo.shape[1]), BF16),
            jax.ShapeDtypeStruct((nb, lc, 128), F32),
            jax.ShapeDtypeStruct((nb, lc, 128), F32),
            jax.ShapeDtypeStruct((nb, ng, lc), F32),
            jax.ShapeDtypeStruct((nb, ng, lc), F32),
        ],
        compiler_params=_params(("arbitrary", "arbitrary"), VMEM_LIMIT),
        name="ml_prep",
    )(xc, mods, norm_g.reshape(1, d), wqk, wv, wo, wg, bg)


CONV_PAD = 72


def _conv_body(main, prev, nxt, cw_ref, cb_ref, q_o, k_o, sc, *, nct, dk):
    i = pl.program_id(1)
    last = pl.num_programs(1) - 1
    p = CONV_PAD
    w = main.shape[-1]
    sc[0:p - GRID_W] = jnp.zeros((p - GRID_W, w), F32)
    sc[p + TM + GRID_W:] = jnp.zeros((sc.shape[0] - p - TM - GRID_W, w), F32)
    sc[p - GRID_W:p] = jnp.where(i > nct, prev[0], 0.0)
    sc[p + TM:p + TM + GRID_W] = jnp.where(jnp.logical_and(i >= nct, i < last), nxt[0], 0.0)
    sc[p:p + TM] = main[0]
    col = lax.broadcasted_iota(I32, (TM, 1), 0) & (GRID_W - 1)

    def finish(acc):
        y = _silu(acc)
        kt = y[:, w // 2:].T
        qs, ks = [], []
        for h in range(HEADS):
            qs += [y[:, h * dk:(h + 1) * dk] * (dk ** -0.5), jnp.zeros((TM, 128 - dk), F32)]
            ks += [kt[h * dk:(h + 1) * dk], jnp.zeros((128 - dk, TM), F32)]
        q_o[0] = jnp.concatenate(qs, axis=1).astype(BF16)
        k_o[0] = jnp.concatenate(ks, axis=0).astype(BF16)

    @pl.when(i < nct)
    def _():
        acc = cb_ref[...] + sc[p:p + TM] * cw_ref[4:5]
        acc += sc[p - 1:p - 1 + TM] * cw_ref[3:4] + sc[p + 1:p + 1 + TM] * cw_ref[5:6]
        finish(acc)

    @pl.when(i >= nct)
    def _():
        acc = jnp.broadcast_to(cb_ref[...], (TM, w))
        for dy in range(3):
            for dx in range(3):
                off = p + (dy - 1) * GRID_W + (dx - 1)
                u = sc[off:off + TM]
                if dx == 0:
                    u = jnp.where(col != 0, u, 0.0)
                if dx == 2:
                    u = jnp.where(col != GRID_W - 1, u, 0.0)
                acc += u * cw_ref[dy * 3 + dx:dy * 3 + dx + 1]
        finish(acc)


def _conv(qk, conv_w, conv_b, nct):
    nb, lc, w = qk.shape
    nt = lc // TM
    per = TM // GRID_W
    nh = lc // GRID_W
    return pl.pallas_call(
        functools.partial(_conv_body, nct=nct, dk=w // 2 // HEADS),
        grid=(nb, nt),
        in_specs=[
            pl.BlockSpec((1, TM, w), lambda b, i: (b, i, 0)),
            pl.BlockSpec((1, GRID_W, w), lambda b, i: (b, jnp.maximum(i * per - 1, 0), 0)),
            pl.BlockSpec((1, GRID_W, w), lambda b, i: (b, jnp.minimum(i * per + per, nh - 1), 0)),
            pl.BlockSpec((9, w), lambda b, i: (0, 0)),
            pl.BlockSpec((1, w), lambda b, i: (0, 0)),
        ],
        out_specs=[pl.BlockSpec((1, TM, HEADS * 128), lambda b, i: (b, i, 0)),
                   pl.BlockSpec((1, HEADS * 128, TM), lambda b, i: (b, 0, i))],
        out_shape=[jax.ShapeDtypeStruct((nb, lc, HEADS * 128), BF16),
                   jax.ShapeDtypeStruct((nb, HEADS * 128, lc), BF16)],
        scratch_shapes=[pltpu.VMEM((CONV_PAD + TM + GRID_W + 8, w), F32)],
        compiler_params=_params(("arbitrary", "arbitrary")),
        name="ml_conv",
    )(qk, qk, qk, conv_w.reshape(9, w), conv_b.reshape(1, w))


ML_CHUNK = 128


def _dot_exact_rhs(x, t):
    p1, p2, p3 = _split3(x)
    return _dot(p1, t) + (_dot(p2, t) + _dot(p3, t))


def _ml_scan_body(qf, ktf, vf, gif, gff, gitf, gftf, qb, ktb, vb, gib, gfb, gitb, gftb,
                  of, ob, c_ref, mrow_ref, mcol_ref):
    c = ML_CHUNK
    nd = 2 * HEADS

    @pl.when(pl.program_id(1) == 0)
    def _():
        for ref in (c_ref, mrow_ref, mcol_ref):
            ref[...] = jnp.zeros_like(ref)

    low = _causal(c, False)
    upp = _causal(c, True)
    low_b, upp_b = low.astype(BF16), upp.astype(BF16)
    neg = -jnp.inf

    lane = lax.broadcasted_iota(I32, (c, 128), 1)
    tpos = lax.broadcasted_iota(I32, (c, 128), 0)
    fwd_l = lane < HEADS
    gi = jnp.where(fwd_l, gif[0], gib[0])
    gf = jnp.where(fwd_l, gff[0], gfb[0])
    bs = jnp.where(fwd_l, _dot_exact_lhs(low_b, gf), _dot_exact_lhs(upp_b, gf))
    a = gi - bs
    pre, suf = a, a
    step = 1
    while step < c:
        pre = jnp.maximum(pre, jnp.where(tpos >= step, pltpu.roll(pre, step, 0), neg))
        suf = jnp.maximum(suf, jnp.where(tpos < c - step, pltpu.roll(suf, c - step, 0), neg))
        step *= 2
    m0 = mrow_ref[0:1]
    mm = jnp.maximum(m0, jnp.where(fwd_l, pre, suf))
    w_inter = jnp.exp(m0 - mm)
    floor = jnp.exp(-(bs + mm))
    fwd_r = fwd_l[0:1]
    b_end = jnp.where(fwd_r, bs[c - 1:c], bs[0:1])
    m_end = jnp.where(fwd_r, mm[c - 1:c], mm[0:1])
    decay = jnp.exp(m0 - m_end)
    mrow_ref[...] = jnp.broadcast_to(b_end + m_end, mrow_ref.shape)

    rowj = lax.broadcasted_iota(I32, (nd, c), 0)
    fwd_t = rowj < HEADS
    git = jnp.where(fwd_t, gitf[0], gitb[0])
    gft = jnp.where(fwd_t, gftf[0], gftb[0])
    bst = jnp.where(fwd_t, _dot_exact_rhs(gft, upp_b), _dot_exact_rhs(gft, low_b))
    at = git - bst
    m0c = mcol_ref[...]
    m_end_c = jnp.maximum(m0c[:, 0:1], jnp.max(at, axis=-1, keepdims=True))
    wk_t = jnp.exp(at - m_end_c)
    b_end_c = jnp.where(fwd_t[:, 0:1], bst[:, c - 1:c], bst[:, 0:1])
    mcol_ref[...] = jnp.broadcast_to(b_end_c + m_end_c, mcol_ref.shape)

    wide = HEADS * 128
    head_of_lane = lax.shift_right_logical(lax.broadcasted_iota(I32, (nd, wide), 1), 7)
    row_w = lax.broadcasted_iota(I32, (nd, wide), 0)
    spos = lax.broadcasted_iota(I32, (c, wide), 1) & 127
    tpos_w = lax.broadcasted_iota(I32, (c, wide), 0)
    pad_rows = jnp.zeros((128 - nd, wide), BF16)
    at_parts = _split3(at)
    mm_parts = _split3(mm)
    ones_blk = jnp.ones((c, 128), BF16)
    dirs = ((0, qf, ktf, vf, of, spos <= tpos_w), (1, qb, ktb, vb, ob, spos >= tpos_w))
    jobs = []
    for d, q_ref, kt_ref, v_ref, o_ref, causal_w in dirs:
        sel = row_w == HEADS * d + head_of_lane
        sel_b = jnp.concatenate([sel.astype(BF16), pad_rows], axis=0)
        tile8 = lambda x: jnp.concatenate([x] * HEADS, axis=1)
        a_rows = [jnp.where(sel, tile8(p), 0).astype(BF16) for p in at_parts]
        zero_rows = jnp.zeros((128 - 3 * nd, wide), BF16)
        rhs = jnp.concatenate(a_rows + [zero_rows, sel_b, sel_b, sel_b], axis=0)
        lhs = jnp.concatenate(
            [(lane < 3 * nd).astype(BF16)] + [(-p).astype(BF16) for p in mm_parts], axis=1)
        w_intra = jnp.where(causal_w, jnp.exp(_dot(lhs, rhs)), 0.0)
        w_inter_w = _dot(w_inter.astype(BF16), sel_b)
        fl_hi, fl_lo = _split2(floor)
        floor_w = _dot(fl_hi, sel_b) + _dot(fl_lo, sel_b)
        dc_hi, dc_lo = _split2(jnp.broadcast_to(decay, (8, 128)))
        decay_w = (_dot(dc_hi, sel_b) + _dot(dc_lo, sel_b))[0:1]
        q = q_ref[0]
        kt = kt_ref[0]
        v = v_ref[0]
        wq = (w_inter_w * q.astype(F32)).astype(BF16)
        for h in range(HEADS):
            sl = slice(h * 128, (h + 1) * 128)
            jobs.append(dict(d=d, h=h, sl=sl, o_ref=o_ref, q=q[:, sl], kt=kt[sl, :], wq=wq[:, sl],
                             w_intra=w_intra[:, sl], floor=floor_w[:, sl], wk=wk_t[HEADS * d + h:HEADS * d + h + 1, :],
                             v_aug=jnp.concatenate([v[:, sl], ones_blk], axis=1),
                             dec=jnp.concatenate([decay_w[:, sl], decay_w[:, sl]], axis=1)))
    for j in jobs:
        j["sc"] = (_dot(j["q"], j["kt"]) * j["w_intra"]).astype(BF16)
    for j in jobs:
        st = c_ref[j["d"], j["h"]]
        out = _dot(jnp.concatenate([j["sc"], j["wq"]], axis=1),
                   jnp.concatenate([j["v_aug"], st.astype(BF16)], axis=0))
        j["o_ref"][0, :, j["sl"]] = out[:, :128] / jnp.maximum(jnp.abs(out[:, 128:]), j["floor"])
    for j in jobs:
        kw_t = (j["kt"].astype(F32) * j["wk"]).astype(BF16)
        c_ref[j["d"], j["h"]] = j["dec"] * c_ref[j["d"], j["h"]] + _dot(kw_t, j["v_aug"])


def _ml_scan(q, kt, v, gi, gf, git, gft, ctx_len):
    nb, lc, wide = q.shape
    dvv = v.shape[-1]
    nc = lc // ML_CHUNK
    fwd, bwd = _chunk_maps(ctx_len // ML_CHUNK, nc)
    swap = lambda m: (lambda b, c: (m(b, c)[0], 0, m(b, c)[1]))
    tm = lambda w, m: pl.BlockSpec((1, ML_CHUNK, w), m)
    ft = lambda r, m: pl.BlockSpec((1, r, ML_CHUNK), swap(m))
    ng = git.shape[1]
    ins = lambda m: [tm(wide, m), ft(wide, m), tm(dvv, m), tm(128, m), tm(128, m), ft(ng, m), ft(ng, m)]
    return pl.pallas_call(
        _ml_scan_body,
        grid=(nb, nc),
        in_specs=ins(fwd) + ins(bwd),
        out_specs=[tm(dvv, fwd), tm(dvv, bwd)],
        out_shape=[jax.ShapeDtypeStruct((nb, lc, dvv), F32)] * 2,
        scratch_shapes=[pltpu.VMEM((2, HEADS, 128, 256), F32), pltpu.VMEM((8, 128), F32),
                        pltpu.VMEM((ng, 128), F32)],
        compiler_params=_params(("arbitrary", "arbitrary"), VMEM_LIMIT),
        name="ml_scan",
    )(q, kt, v, gi, gf, git, gft, q, kt, v, gi, gf, git, gft)


def kernel(x, c, ctx, c_ctx, ada_w, ada_b, norm_mix_g, norm_ffn_g, hg_w_in, hg_lb_logits, hg_norm_g, hg_w_out,
           ml_w_in, ml_b_gate, ml_conv_w, ml_conv_b, ml_norm_g, ml_w_out, router_w, router_b,
           moe_w1, moe_b1, moe_w2, moe_b2, final_norm_g):
    nb, seq, d = x.shape
    ctx_len = ctx.shape[1]
    depth = ada_w.shape[0]
    assert ctx_len == TM and seq % TM == 0 and seq % GRID_W == 0 and d % (HEADS * 128) == 0
    nct = ctx_len // TM

    xc = jnp.concatenate([ctx, x], axis=1)
    rows = -(-(nb + 1) // 8) * 8
    cstack = jnp.concatenate([c, c_ctx[None, :], jnp.zeros((rows - nb - 1, d), F32)], axis=0)
    mods = _ada(cstack, ada_w, ada_b).reshape(depth, rows, 6, d)

    for layer in range(depth):
        j = layer // 2
        last = layer == depth - 1
        if layer % 2 == 0:
            q, vt, kf, kb, lf, lb, g = _hg_prep(xc, mods, norm_mix_g[layer], hg_w_in[j].astype(BF16),
                                                hg_lb_logits, layer, nct)
            of, ob = _hg_scan(q, vt, kf, kb, lf, lb, ctx_len)
            norm_g, w_out, gate = hg_norm_g[j], hg_w_out[j], "silu"
        else:
            w = ml_w_in[j]
            dqk = ml_conv_w.shape[-1]
            dv = ml_w_out.shape[1]
            qk, v, g, gi, gf, git, gft = _ml_prep(
                xc, mods, norm_mix_g[layer], w[:, :dqk].astype(BF16), w[:, dqk:dqk + dv].astype(BF16),
                w[:, dqk + dv:dqk + 2 * dv].astype(BF16), w[:, dqk + 2 * dv:], ml_b_gate[j][None, :], layer, nct)
            q, kt = _conv(qk, ml_conv_w[j], ml_conv_b[j], nct)
            of, ob = _ml_scan(q, kt, v, gi, gf, git, gft, ctx_len)
            norm_g, w_out, gate = ml_norm_g[j], ml_w_out[j], "sigmoid"
        skip = nct if last else 0
        xc, f, ti, tg, cnt = _post(of, ob, g, xc, mods, norm_g, w_out.astype(BF16), norm_ffn_g[layer],
                                   router_w[layer], router_b[layer], layer, nct, skip, gate)
        xc = _moe(f, ti, tg, cnt, xc, mods, moe_w1, moe_b1, moe_w2, moe_b2, final_norm_g, layer, nct, skip, last)
    return xc
```
